```python
import math
import jax, jax.numpy as jnp
from jax import lax
import numpy as np

D_MODEL = 1024
BATCH = 2
SEQ = 8192
DEPTH = 1
DEC_BATCH = 32
DEC_SEQ = 8
PAST_LEN = 8192
PAGE_SIZE = 128

N_HEADS = 8
DK = 64
DV = 2 * DK
D_ATTN = N_HEADS * DV
D_CONV = D_MODEL
CONV_W = 31
Q_BLOCK = 128
EPS = 1e-6

Q_END = N_HEADS * 2 * DK
K_END = Q_END + N_HEADS * 2 * DK
V_END = K_END + D_ATTN
GA_END = V_END + D_ATTN
GLU_END = GA_END + 2 * D_CONV
GC_END = GLU_END + D_CONV
D_IN = GC_END + 2 * D_MODEL

kernel_name = 'hybrid_diffattn_conformer_conv_decode_step'


def rmsnorm(x, g):
    xf = x.astype(jnp.float32)
    y = xf * lax.rsqrt(jnp.mean(xf * xf, axis=-1, keepdims=True) + EPS)
    return (y * g.astype(jnp.float32)).astype(x.dtype)


def layernorm(x, g, b):
    xf = x.astype(jnp.float32)
    mu = jnp.mean(xf, axis=-1, keepdims=True)
    var = jnp.mean(jnp.square(xf - mu), axis=-1, keepdims=True)
    y = (xf - mu) * lax.rsqrt(var + EPS) * g.astype(jnp.float32) + b.astype(jnp.float32)
    return y.astype(x.dtype)


def mixer_inputs(x, c, w_ada, b_ada, norm_g, w_in):
    mod = jax.nn.silu(c) @ w_ada + b_ada
    shift, scale, gate = jnp.split(mod, 3, axis=-1)
    h = rmsnorm(x, norm_g) * (1.0 + scale[:, None, :]) + shift[:, None, :]
    z = h @ w_in
    B, T = x.shape[0], x.shape[1]
    q = z[..., :Q_END].reshape(B, T, N_HEADS, 2, DK)
    k = z[..., Q_END:K_END].reshape(B, T, N_HEADS, 2, DK)
    v = z[..., K_END:V_END].reshape(B, T, N_HEADS, DV)
    ga = z[..., V_END:GA_END]
    glu_a = z[..., GA_END:GA_END + D_CONV]
    glu_b = z[..., GA_END + D_CONV:GLU_END]
    u = glu_a * jax.nn.sigmoid(glu_b)
    gc = z[..., GLU_END:GC_END]
    gm = z[..., GC_END:]
    return q, k, v, ga, u, gc, gm, gate


def diff_probs(s, lam):
    p = jax.nn.softmax(s, axis=-1)
    return p[0] - lam * p[1]


def prompt_attention(q, k, v, lam):
    B, S = q.shape[0], q.shape[1]
    nb = S // Q_BLOCK
    qb = q.reshape(B, nb, Q_BLOCK, N_HEADS, 2, DK).transpose(1, 0, 2, 3, 4, 5)
    k_pos = jnp.arange(S)
    scale = DK ** -0.5

    def block(args):
        qi, i = args
        q_pos = i * Q_BLOCK + jnp.arange(Q_BLOCK)
        mask = k_pos[None, :] <= q_pos[:, None]
        s = jnp.einsum('bqhmd,bkhmd->mbhqk', qi, k).astype(jnp.float32) * scale
        s = jnp.where(mask, s, -jnp.inf)
        p = diff_probs(s, lam).astype(v.dtype)
        return jnp.einsum('bhqk,bkhd->bqhd', p, v)

    o = lax.map(block, (qb, jnp.arange(nb)))
    return o.transpose(1, 0, 2, 3, 4).reshape(B, S, N_HEADS, DV)


def sample_attention(q, k, v, k_cache, v_cache, page_table, lam):
    Bd, T = q.shape[0], q.shape[1]
    P = page_table.shape[1] * PAGE_SIZE
    kc = k_cache[page_table].reshape(Bd, P, N_HEADS, 2, DK)
    vc = v_cache[page_table].reshape(Bd, P, N_HEADS, DV)
    scale = DK ** -0.5
    causal = jnp.tril(jnp.ones((T, T), dtype=bool))
    s_past = jnp.einsum('bqhmd,bkhmd->mbhqk', q, kc).astype(jnp.float32)
    s_new = jnp.einsum('bqhmd,bkhmd->mbhqk', q, k).astype(jnp.float32)
    s_new = jnp.where(causal, s_new, -jnp.inf)
    s = jnp.concatenate([s_past, s_new], axis=-1) * scale
    p = diff_probs(s, lam).astype(v.dtype)
    return (jnp.einsum('bhqk,bkhd->bqhd', p[..., :P], vc)
            + jnp.einsum('bhqk,bkhd->bqhd', p[..., P:], v))


def attn_branch_out(o, lam_init, subln_g, ga, w_attn_o):
    B, T = o.shape[0], o.shape[1]
    o = rmsnorm(o, subln_g) * (1.0 - lam_init)
    o = o.reshape(B, T, D_ATTN) * jax.nn.silu(ga)
    return o @ w_attn_o


def conv_branch_out(u_ext, gc, conv_w, conv_b, ln_g, ln_b, w_conv_o, b_conv_o):
    y = lax.conv_general_dilated(u_ext, conv_w[:, None, :], window_strides=(1,), padding='VALID',
                                 dimension_numbers=('NWC', 'WIO', 'NWC'),
                                 feature_group_count=D_CONV) + conv_b
    y = jax.nn.silu(layernorm(y, ln_g, ln_b)) * jax.nn.silu(gc)
    return y @ w_conv_o + b_conv_o


def merge_residual(x, ya, yc, gm, gate, w_out):
    g_a = jax.nn.sigmoid(gm[..., :D_MODEL])
    g_c = jax.nn.sigmoid(gm[..., D_MODEL:])
    m = g_a * ya + g_c * yc
    return x + gate[:, None, :] * (m @ w_out)


def setup_inputs(seed: int = 0) -> dict:
    key = jax.random.key(seed)
    ks = jax.random.split(key, 32)
    f32 = jnp.float32
    n_pages = PAST_LEN // PAGE_SIZE
    n_phys = (DEC_BATCH * n_pages * 5) // 4

    def nrm(k, shape, s):
        return jax.random.normal(k, shape, f32) * s

    L = DEPTH
    perm = jax.random.permutation(ks[6], n_phys)
    page_table = perm[:DEC_BATCH * n_pages].reshape(DEC_BATCH, n_pages).astype(jnp.int32)
    return {
        'x_prompt': nrm(ks[0], (BATCH, SEQ, D_MODEL), 1.0),
        'x_sample': nrm(ks[1], (DEC_BATCH, DEC_SEQ, D_MODEL), 1.0),
        'cache_k': nrm(ks[2], (L, n_phys, PAGE_SIZE, N_HEADS, 2 * DK), 1.0),
        'cache_v': nrm(ks[3], (L, n_phys, PAGE_SIZE, N_HEADS, DV), 1.0),
        'state_conv': nrm(ks[4], (L, DEC_BATCH, CONV_W - 1, D_CONV), 0.5),
        'page_table': page_table,
        'c_prompt': nrm(ks[5], (BATCH, D_MODEL), 1.0),
        'c_sample': nrm(ks[7], (DEC_BATCH, D_MODEL), 1.0),
        'w_ada': nrm(ks[8], (L, D_MODEL, 3 * D_MODEL), 0.1 * D_MODEL ** -0.5),
        'b_ada': nrm(ks[9], (L, 3 * D_MODEL), 0.01),
        'norm_g': 1.0 + nrm(ks[10], (L, D_MODEL), 0.01),
        'w_in': nrm(ks[11], (L, D_MODEL, D_IN), D_MODEL ** -0.5),
        'lambda_q1': nrm(ks[12], (L, DK), 0.1),
        'lambda_k1': nrm(ks[13], (L, DK), 0.1),
        'lambda_q2': nrm(ks[14], (L, DK), 0.1),
        'lambda_k2': nrm(ks[15], (L, DK), 0.1),
        'subln_g': 1.0 + nrm(ks[16], (L, DV), 0.01),
        'w_attn_o': nrm(ks[17], (L, D_ATTN, D_MODEL), D_ATTN ** -0.5),
        'conv_w': nrm(ks[18], (L, CONV_W, D_CONV), CONV_W ** -0.5),
        'conv_b': nrm(ks[19], (L, D_CONV), 0.01),
        'conv_ln_g': 1.0 + nrm(ks[20], (L, D_CONV), 0.01),
        'conv_ln_b': nrm(ks[21], (L, D_CONV), 0.01),
        'w_conv_o': nrm(ks[22], (L, D_CONV, D_MODEL), D_CONV ** -0.5),
        'b_conv_o': nrm(ks[23], (L, D_MODEL), 0.01),
        'w_out': nrm(ks[24], (L, D_MODEL, D_MODEL), D_MODEL ** -0.5),
        'final_g': 1.0 + nrm(ks[25], (D_MODEL,), 0.01),
    }


def reference(x_prompt, x_sample, cache_k, cache_v, state_conv, page_table, c_prompt, c_sample,
              w_ada, b_ada, norm_g, w_in, lambda_q1, lambda_k1, lambda_q2, lambda_k2, subln_g,
              w_attn_o, conv_w, conv_b, conv_ln_g, conv_ln_b, w_conv_o, b_conv_o, w_out, final_g):
    xp, xs = x_prompt, x_sample
    kp_l, vp_l, cp_l, ks_l, vs_l, cs_l = [], [], [], [], [], []
    for l in range(DEPTH):
        lam_init = 0.8 - 0.6 * math.exp(-0.3 * l)
        lam = (jnp.exp(jnp.sum(lambda_q1[l].astype(jnp.float32) * lambda_k1[l].astype(jnp.float32)))
               - jnp.exp(jnp.sum(lambda_q2[l].astype(jnp.float32) * lambda_k2[l].astype(jnp.float32)))
               + lam_init)

        q, k, v, ga, u, gc, gm, gate = mixer_inputs(xp, c_prompt, w_ada[l], b_ada[l], norm_g[l], w_in[l])
        o = prompt_attention(q, k, v, lam)
        ya = attn_branch_out(o, lam_init, subln_g[l], ga, w_attn_o[l])
        u_ext = jnp.concatenate([jnp.zeros((u.shape[0], CONV_W - 1, D_CONV), u.dtype), u], axis=1)
        yc = conv_branch_out(u_ext, gc, conv_w[l], conv_b[l], conv_ln_g[l], conv_ln_b[l], w_conv_o[l], b_conv_o[l])
        xp = merge_residual(xp, ya, yc, gm, gate, w_out[l])
        kp_l.append(k.reshape(k.shape[0], k.shape[1], N_HEADS, 2 * DK))
        vp_l.append(v)
        cp_l.append(u_ext[:, -(CONV_W - 1):])

        q, k, v, ga, u, gc, gm, gate = mixer_inputs(xs, c_sample, w_ada[l], b_ada[l], norm_g[l], w_in[l])
        o = sample_attention(q, k, v, cache_k[l], cache_v[l], page_table, lam)
        ya = attn_branch_out(o, lam_init, subln_g[l], ga, w_attn_o[l])
        u_ext = jnp.concatenate([state_conv[l].astype(u.dtype), u], axis=1)
        yc = conv_branch_out(u_ext, gc, conv_w[l], conv_b[l], conv_ln_g[l], conv_ln_b[l], w_conv_o[l], b_conv_o[l])
        xs = merge_residual(xs, ya, yc, gm, gate, w_out[l])
        ks_l.append(k.reshape(k.shape[0], k.shape[1], N_HEADS, 2 * DK))
        vs_l.append(v)
        cs_l.append(u_ext[:, -(CONV_W - 1):])

    y_prompt = rmsnorm(xp, final_g)
    y_sample = rmsnorm(xs, final_g)
    k_prompt = jnp.stack(kp_l)
    v_prompt = jnp.stack(vp_l)
    conv_prompt = jnp.stack(cp_l)
    k_sample = jnp.stack(ks_l)
    v_sample = jnp.stack(vs_l)
    conv_sample = jnp.stack(cs_l)
    return (y_prompt, y_sample, k_prompt, v_prompt, conv_prompt, k_sample, v_sample, conv_sample)
```

```python
import functools
import math

import jax
import jax.numpy as jnp
from jax import lax
from jax.experimental import pallas as pl
from jax.experimental.pallas import tpu as pltpu

F32 = jnp.float32
BF16 = jnp.bfloat16

D_MODEL = 1024
N_HEADS = 8
DK = 64
DV = 2 * DK
CONV_W = 31
PAGE = 128
EPS = 1e-6
N_SEG = 9
D_IN = N_SEG * D_MODEL
LAM_INIT = 0.8 - 0.6 * math.exp(-0.3 * 0)
LOG2E = 1.4426950408889634
Q_SCALE = DK ** -0.5 * LOG2E

HALO = 32
VMEM_LIMIT = 56 * 1024 * 1024


def _silu(x):
    return x * jax.nn.sigmoid(x)


def _modulated_norm(x, g, scale, shift):
    ms = jnp.mean(x * x, axis=-1, keepdims=True)
    return (x * lax.rsqrt(ms + EPS) * g) * (1.0 + scale) + shift


def _layernorm_gate(y, ln_g, ln_b, gc):
    mu = jnp.mean(y, axis=-1, keepdims=True)
    yc = y - mu
    var = jnp.mean(yc * yc, axis=-1, keepdims=True)
    ln = yc * lax.rsqrt(var + EPS) * ln_g + ln_b
    return _silu(ln) * _silu(gc)


def _conv_rows(blk, cw, rc):
    base = HALO - (CONV_W - 1)
    acc = None
    for r in range(8):
        sh = blk if r == 0 else blk[r:r + rc + HALO - 8]
        for a in range(HALO // 8 + 1):
            w = 8 * a + r - base
            if 0 <= w < CONV_W:
                term = sh[8 * a:8 * a + rc] * cw[w:w + 1]
                acc = term if acc is None else acc + term
    return acc


def _lam(lamp_ref):
    lp = lamp_ref[...]
    a = jnp.sum(lp[0:1] * lp[1:2], axis=-1, keepdims=True)
    b = jnp.sum(lp[2:3] * lp[3:4], axis=-1, keepdims=True)
    return jnp.exp(a) - jnp.exp(b) + LAM_INIT


def _subln(a, sg):
    ms = jnp.mean(a * a, axis=-1, keepdims=True)
    return a * lax.rsqrt(ms + EPS) * sg * (1.0 - LAM_INIT)


def _ada_kernel(c_ref, w_ref, b_ref, o_ref):
    a = _silu(c_ref[...]).astype(BF16)
    o_ref[...] = jnp.dot(a, w_ref[...].astype(BF16), preferred_element_type=F32) + b_ref[...]


def _ada(c_all, w_ada, b_ada):
    rows = c_all.shape[0]
    tn = 768
    return pl.pallas_call(
        _ada_kernel,
        grid=(3 * D_MODEL // tn,),
        in_specs=[pl.BlockSpec((rows, D_MODEL), lambda j: (0, 0)),
                  pl.BlockSpec((D_MODEL, tn), lambda j: (0, j)),
                  pl.BlockSpec((1, tn), lambda j: (0, j))],
        out_specs=pl.BlockSpec((rows, tn), lambda j: (0, j)),
        out_shape=jax.ShapeDtypeStruct((rows, 3 * D_MODEL), F32),
        name="ada",
    )(c_all, w_ada, b_ada)


def _front_kernel(x_ref, scale_ref, shift_ref, g_ref, w_ref, cw_ref, cb_ref, lng_ref, lnb_ref,
                  q_ref, k_ref, kb_ref, v_ref, vb_ref, ga_ref, ca_ref, gm_ref, cst_ref,
                  h_scr, uext_scr, y_scr, *, tm):
    t = pl.program_id(1)
    h_scr[...] = _modulated_norm(x_ref[0], g_ref[...], scale_ref[0], shift_ref[0]).astype(BF16)

    def seg(j, n=1):
        return jnp.dot(h_scr[...], w_ref[:, j * D_MODEL:(j + n) * D_MODEL], preferred_element_type=F32)

    q_ref[0] = (seg(0) * Q_SCALE).astype(BF16)
    zk = seg(1)
    k_ref[0] = zk
    kb_ref[0] = zk.astype(BF16)
    zv = seg(2)
    v_ref[0] = zv
    vb_ref[0] = zv.astype(BF16)
    ga_ref[0] = seg(3).astype(BF16)
    gm_ref[0] = seg(7, 2).astype(BF16)

    @pl.when(t == 0)
    def _():
        uext_scr[0:HALO, :] = jnp.zeros((HALO, D_MODEL), F32)

    uext_scr[HALO:HALO + tm, :] = seg(4) * jax.nn.sigmoid(seg(5))

    rc = 32

    def conv_rows(ci, carry):
        r0 = pl.multiple_of(ci * rc, rc)
        for lc in range(D_MODEL // 128):
            ls = slice(lc * 128, (lc + 1) * 128)
            y_scr[pl.ds(r0, rc), ls] = _conv_rows(uext_scr[pl.ds(r0, rc + HALO), ls], cw_ref[:, ls], rc)
        return carry

    lax.fori_loop(0, tm // rc, conv_rows, 0)

    last = uext_scr[tm:tm + HALO, :]
    uext_scr[0:HALO, :] = last

    @pl.when(t == pl.num_programs(1) - 1)
    def _():
        cst_ref[0] = last[HALO - (CONV_W - 1):HALO]

    ca_ref[0] = _layernorm_gate(y_scr[...] + cb_ref[...], lng_ref[...], lnb_ref[...], seg(6)).astype(BF16)


def _front(x, scale, shift, norm_g, w_in, conv_w, conv_b, ln_g, ln_b, tm):
    B, S, _ = x.shape
    tile = lambda width: pl.BlockSpec((1, tm, width), lambda b, t: (b, t, 0))
    row = lambda: pl.BlockSpec((1, 1, D_MODEL), lambda b, t: (b, 0, 0))
    const = lambda shape: pl.BlockSpec(shape, lambda b, t: (0,) * len(shape))
    bsd = lambda width, dt: jax.ShapeDtypeStruct((B, S, width), dt)
    return pl.pallas_call(
        functools.partial(_front_kernel, tm=tm),
        grid=(B, S // tm),
        in_specs=[tile(D_MODEL), row(), row(), const((1, D_MODEL)),
                  pl.BlockSpec((D_MODEL, D_IN), lambda b, t: (0, 0), pipeline_mode=pl.Buffered(1)),
                  const((CONV_W, D_MODEL)), const((1, D_MODEL)), const((1, D_MODEL)), const((1, D_MODEL))],
        out_specs=[tile(D_MODEL), tile(D_MODEL), tile(D_MODEL), tile(D_MODEL), tile(D_MODEL),
                   tile(D_MODEL), tile(D_MODEL), tile(2 * D_MODEL),
                   pl.BlockSpec((1, CONV_W - 1, D_MODEL), lambda b, t: (b, 0, 0))],
        out_shape=[bsd(D_MODEL, BF16), bsd(D_MODEL, F32), bsd(D_MODEL, BF16), bsd(D_MODEL, F32),
                   bsd(D_MODEL, BF16), bsd(D_MODEL, BF16), bsd(D_MODEL, BF16), bsd(2 * D_MODEL, BF16),
                   jax.ShapeDtypeStruct((B, CONV_W - 1, D_MODEL), F32)],
        scratch_shapes=[pltpu.VMEM((tm, D_MODEL), BF16),
                        pltpu.VMEM((HALO + tm, D_MODEL), F32),
                        pltpu.VMEM((tm, D_MODEL), F32)],
        compiler_params=pltpu.CompilerParams(dimension_semantics=("arbitrary", "arbitrary"),
                                             vmem_limit_bytes=VMEM_LIMIT),
        name="front",
    )(x, scale, shift, norm_g, w_in, conv_w, conv_b, ln_g, ln_b)


def _attn_kernel(lamp_ref, q_ref, k_ref, v_ref, sg_ref, o_ref, qbd_scr, m_scr, l_scr, acc_scr, *, tq):
    i = pl.program_id(2)
    q = q_ref[0]
    lane = lax.broadcasted_iota(jnp.int32, (tq, DV), 1)
    zero = jnp.zeros_like(q)
    qbd_scr[0:tq, :] = jnp.where(lane < DK, q, zero)
    qbd_scr[tq:2 * tq, :] = jnp.where(lane >= DK, q, zero)
    m_scr[...] = jnp.full((2 * tq, 1), -jnp.inf, F32)
    l_scr[...] = jnp.zeros((2 * tq, 1), F32)
    acc_scr[...] = jnp.zeros((2 * tq, DV), F32)

    def step(j, diagonal):
        k0 = pl.multiple_of(j * tq, tq)
        k = k_ref[0, pl.ds(k0, tq), :]
        v = v_ref[0, pl.ds(k0, tq), :]
        s = lax.dot_general(qbd_scr[...], k, (((1,), (1,)), ((), ())), preferred_element_type=F32)
        if diagonal:
            row = lax.broadcasted_iota(jnp.int32, (2 * tq, tq), 0)
            col = lax.broadcasted_iota(jnp.int32, (2 * tq, tq), 1)
            s = jnp.where(col <= jnp.where(row >= tq, row - tq, row), s, -jnp.inf)
        m_prev = m_scr[...]
        m_new = jnp.maximum(m_prev, jnp.max(s, axis=1, keepdims=True))
        alpha = jnp.exp2(m_prev - m_new)
        p = jnp.exp2(s - m_new)
        l_scr[...] = alpha * l_scr[...] + jnp.sum(p, axis=1, keepdims=True)
        acc_scr[...] = alpha * acc_scr[...] + jnp.dot(p.astype(BF16), v, preferred_element_type=F32)
        m_scr[...] = m_new

    def full_step(j, carry):
        step(j, False)
        return carry

    lax.fori_loop(0, i, full_step, 0)
    step(i, True)

    o = acc_scr[...] / l_scr[...]
    o_ref[0] = _subln(o[0:tq] - _lam(lamp_ref) * o[tq:2 * tq], sg_ref[...]).astype(BF16)


def _attention(lamp, q, kb, vb, subln_g, tq):
    B, S, _ = q.shape
    return pl.pallas_call(
        functools.partial(_attn_kernel, tq=tq),
        grid=(B, N_HEADS, S // tq),
        in_specs=[pl.BlockSpec((4, DK), lambda b, h, i: (0, 0)),
                  pl.BlockSpec((1, tq, DV), lambda b, h, i: (b, i, h)),
                  pl.BlockSpec((1, S, DV), lambda b, h, i: (b, 0, h)),
                  pl.BlockSpec((1, S, DV), lambda b, h, i: (b, 0, h)),
                  pl.BlockSpec((1, DV), lambda b, h, i: (0, 0))],
        out_specs=pl.BlockSpec((1, tq, DV), lambda b, h, i: (b, i, h)),
        out_shape=jax.ShapeDtypeStruct((B, S, N_HEADS * DV), BF16),
        scratch_shapes=[pltpu.VMEM((2 * tq, DV), BF16),
                        pltpu.VMEM((2 * tq, 1), F32),
                        pltpu.VMEM((2 * tq, 1), F32),
                        pltpu.VMEM((2 * tq, DV), F32)],
        compiler_params=pltpu.CompilerParams(dimension_semantics=("arbitrary", "arbitrary", "arbitrary"),
                                             vmem_limit_bytes=VMEM_LIMIT),
        name="attn",
    )(lamp, q, kb, vb, subln_g)


def _sproj_kernel(x_ref, scale_ref, shift_ref, g_ref, w_ref, z_ref, h_scr):
    @pl.when(pl.program_id(0) == 0)
    def _():
        h_scr[...] = _modulated_norm(x_ref[...], g_ref[...], scale_ref[...], shift_ref[...]).astype(BF16)

    z_ref[...] = jnp.dot(h_scr[...], w_ref[...], preferred_element_type=F32)


def _sproj(x, scale, shift, norm_g, w_in):
    n = x.shape[0]
    full = pl.BlockSpec((n, D_MODEL), lambda j: (0, 0))
    return pl.pallas_call(
        _sproj_kernel,
        grid=(N_SEG,),
        in_specs=[full, full, full, pl.BlockSpec((1, D_MODEL), lambda j: (0, 0)),
                  pl.BlockSpec((D_MODEL, D_MODEL), lambda j: (0, j))],
        out_specs=pl.BlockSpec((n, D_MODEL), lambda j: (0, j)),
        out_shape=jax.ShapeDtypeStruct((n, D_IN), F32),
        scratch_shapes=[pltpu.VMEM((n, D_MODEL), BF16)],
        compiler_params=pltpu.CompilerParams(dimension_semantics=("arbitrary",)),
        name="sproj",
    )(x, scale, shift, norm_g, w_in)


def _sconv_kernel(za_ref, zb_ref, zc_ref, st_ref, cw_ref, cb_ref, lng_ref, lnb_ref,
                  ca_ref, cst_ref, u_scr, uext_scr, y_scr, *, n_seq, t_new):
    u_scr[...] = za_ref[...] * jax.nn.sigmoid(zb_ref[...])

    def one_seq(b, carry):
        r0 = pl.multiple_of(b * t_new, t_new)
        uext_scr[0:HALO, :] = st_ref[b]
        uext_scr[HALO:HALO + t_new, :] = u_scr[pl.ds(r0, t_new), :]
        blk = uext_scr[...]
        y_scr[pl.ds(r0, t_new), :] = _conv_rows(blk, cw_ref[...], t_new)
        cst_ref[b] = blk[HALO + t_new - (CONV_W - 1):HALO + t_new]
        return carry

    lax.fori_loop(0, n_seq, one_seq, 0)
    ca_ref[...] = _layernorm_gate(y_scr[...] + cb_ref[...], lng_ref[...], lnb_ref[...],
                                  zc_ref[...]).astype(BF16)


def _sconv(z, state, conv_w, conv_b, ln_g, ln_b, n_seq, t_new):
    n = n_seq * t_new
    zseg = lambda j: pl.BlockSpec((n, D_MODEL), lambda i: (0, j))
    const = lambda shape: pl.BlockSpec(shape, lambda i: (0,) * len(shape))
    return pl.pallas_call(
        functools.partial(_sconv_kernel, n_seq=n_seq, t_new=t_new),
        grid=(1,),
        in_specs=[zseg(4), zseg(5), zseg(6), const((n_seq, HALO, D_MODEL)),
                  const((CONV_W, D_MODEL)), const((1, D_MODEL)), const((1, D_MODEL)), const((1, D_MODEL))],
        out_specs=[const((n, D_MODEL)), const((n_seq, CONV_W - 1, D_MODEL))],
        out_shape=[jax.ShapeDtypeStruct((n, D_MODEL), BF16),
                   jax.ShapeDtypeStruct((n_seq, CONV_W - 1, D_MODEL), F32)],
        scratch_shapes=[pltpu.VMEM((n, D_MODEL), F32),
                        pltpu.VMEM((HALO + t_new, D_MODEL), F32),
                        pltpu.VMEM((n, D_MODEL), F32)],
        name="sconv",
    )(z, z, z, state, conv_w, conv_b, ln_g, ln_b)


def _decode_kernel(pt_ref, lamp_ref, q_ref, kn_ref, vn_ref, sg_ref, *rest, pp, t_new):
    del pt_ref
    k_refs, v_refs = rest[:pp], rest[pp:2 * pp]
    o_ref, qrow_scr, bias_scr, m_scr, l_scr, acc_scr = rest[2 * pp:]
    step = pl.program_id(1)
    n_rows = N_HEADS * 2 * t_new
    page_rows = PAGE * N_HEADS
    new_rows = t_new * N_HEADS
    log2_heads = N_HEADS.bit_length() - 1
    log2_pair = (2 * t_new).bit_length() - 1
    contract_last = (((1,), (1,)), ((), ()))

    def online_update(s, v, first):
        s_max = jnp.max(s, axis=1, keepdims=True)
        if first:
            m_new = s_max
            p = jnp.exp2(s - m_new)
            l_scr[...] = jnp.sum(p, axis=1, keepdims=True)
            acc_scr[...] = jnp.dot(p.astype(BF16), v, preferred_element_type=F32)
        else:
            m_prev = m_scr[...]
            m_new = jnp.maximum(m_prev, s_max)
            alpha = jnp.exp2(m_prev - m_new)
            p = jnp.exp2(s - m_new)
            l_scr[...] = alpha * l_scr[...] + jnp.sum(p, axis=1, keepdims=True)
            acc_scr[...] = alpha * acc_scr[...] + jnp.dot(p.astype(BF16), v, preferred_element_type=F32)
        m_scr[...] = m_new

    @pl.when(step == 0)
    def _():
        q = q_ref[...] * Q_SCALE
        lane = lax.broadcasted_iota(jnp.int32, (t_new, DV), 1)
        rows = []
        for h in range(N_HEADS):
            qh = q[:, h * DV:(h + 1) * DV]
            rows += [jnp.where(lane < DK, qh, 0.0), jnp.where(lane >= DK, qh, 0.0)]
        qrow_scr[...] = jnp.concatenate(rows, axis=0).astype(BF16)

        row = lax.broadcasted_iota(jnp.int32, (n_rows, page_rows), 0)
        col = lax.broadcasted_iota(jnp.int32, (n_rows, page_rows), 1)
        same_head = (col & (N_HEADS - 1)) == (row >> log2_pair)
        bias_scr[...] = jnp.where(same_head, 0.0, -jnp.inf)

        pad = jnp.zeros((PAGE - new_rows, DV), F32)
        kn = jnp.concatenate([kn_ref[...], pad], axis=0).astype(BF16)
        vn = jnp.concatenate([vn_ref[...], pad], axis=0).astype(BF16)
        s = lax.dot_general(qrow_scr[...], kn, contract_last, preferred_element_type=F32)
        row = lax.broadcasted_iota(jnp.int32, (n_rows, PAGE), 0)
        col = lax.broadcasted_iota(jnp.int32, (n_rows, PAGE), 1)
        visible = ((col & (N_HEADS - 1)) == (row >> log2_pair)) & ((col >> log2_heads) <= (row & (t_new - 1)))
        online_update(jnp.where(visible, s, -jnp.inf), vn, True)

    q_rows = qrow_scr[...]
    bias = bias_scr[...]
    s = jnp.concatenate(
        [lax.dot_general(q_rows, r[0].astype(BF16), contract_last, preferred_element_type=F32) + bias
         for r in k_refs], axis=1)
    v = jnp.concatenate([r[0] for r in v_refs], axis=0).astype(BF16)
    online_update(s, v, False)

    @pl.when(step == pl.num_programs(1) - 1)
    def _():
        lam = _lam(lamp_ref)
        o = acc_scr[...] / l_scr[...]
        for h in range(N_HEADS):
            r = h * 2 * t_new
            a = o[r:r + t_new] - lam * o[r + t_new:r + 2 * t_new]
            o_ref[:, h * DV:(h + 1) * DV] = _subln(a, sg_ref[...])


def _decode(page_table, lamp, q, k_new, v_new, cache_k, cache_v, subln_g, n_seq, t_new, pp):
    n_pages = page_table.shape[1]
    width = N_HEADS * DV
    page_rows = PAGE * N_HEADS
    new_rows = t_new * N_HEADS
    n_rows = N_HEADS * 2 * t_new
    assert t_new & (t_new - 1) == 0 and N_HEADS & (N_HEADS - 1) == 0
    assert n_pages % pp == 0 and new_rows <= PAGE
    page = lambda i: pl.BlockSpec((1, page_rows, DV), lambda b, s, pt: (pt[b, s * pp + i], 0, 0))
    new = lambda: pl.BlockSpec((new_rows, DV), lambda b, s, pt: (b, 0))
    grid_spec = pltpu.PrefetchScalarGridSpec(
        num_scalar_prefetch=1,
        grid=(n_seq, n_pages // pp),
        in_specs=[pl.BlockSpec((4, DK), lambda b, s, pt: (0, 0)),
                  pl.BlockSpec((t_new, width), lambda b, s, pt: (b, 0)), new(), new(),
                  pl.BlockSpec((1, DV), lambda b, s, pt: (0, 0))]
                 + [page(i) for i in range(pp)] + [page(i) for i in range(pp)],
        out_specs=pl.BlockSpec((t_new, width), lambda b, s, pt: (b, 0)),
        scratch_shapes=[pltpu.VMEM((n_rows, DV), BF16),
                        pltpu.VMEM((n_rows, page_rows), F32),
                        pltpu.VMEM((n_rows, 1), F32),
                        pltpu.VMEM((n_rows, 1), F32),
                        pltpu.VMEM((n_rows, DV), F32)])
    return pl.pallas_call(
        functools.partial(_decode_kernel, pp=pp, t_new=t_new),
        grid_spec=grid_spec,
        out_shape=jax.ShapeDtypeStruct((n_seq * t_new, width), F32),
        compiler_params=pltpu.CompilerParams(dimension_semantics=("arbitrary", "arbitrary"),
                                             vmem_limit_bytes=VMEM_LIMIT),
        name="decode",
    )(page_table, lamp, q, k_new, v_new, subln_g, *([cache_k] * pp), *([cache_v] * pp))


def _post_kernel(o_ref, ga_ref, ca_ref, gma_ref, gmc_ref, x_ref, gate_ref,
                 wa_ref, wc_ref, bc_ref, wo_ref, fg_ref, y_ref):
    a = (o_ref[0].astype(F32) * _silu(ga_ref[0].astype(F32))).astype(BF16)
    ya = jnp.dot(a, wa_ref[...], preferred_element_type=F32)
    yc = jnp.dot(ca_ref[0], wc_ref[...], preferred_element_type=F32) + bc_ref[...]
    m = (jax.nn.sigmoid(gma_ref[0].astype(F32)) * ya + jax.nn.sigmoid(gmc_ref[0].astype(F32)) * yc)
    d = jnp.dot(m.astype(BF16), wo_ref[...], preferred_element_type=F32)
    xo = x_ref[0] + gate_ref[0] * d
    ms = jnp.mean(xo * xo, axis=-1, keepdims=True)
    y_ref[0] = xo * lax.rsqrt(ms + EPS) * fg_ref[...]


def _post(o, ga, ga_seg, ca, gm, gm_seg, x, gate, w_attn_o, w_conv_o, b_conv_o, w_out, final_g, tm):
    B, S, _ = x.shape
    gate_rows = gate.shape[1]
    tile = lambda seg=0: pl.BlockSpec((1, tm, D_MODEL), lambda b, t: (b, t, seg))
    const = lambda shape: pl.BlockSpec(shape, lambda b, t: (0,) * len(shape))
    if gate_rows == 1:
        gate_spec = pl.BlockSpec((1, 1, D_MODEL), lambda b, t: (b, 0, 0))
    else:
        gate_spec = tile()
    return pl.pallas_call(
        _post_kernel,
        grid=(B, S // tm),
        in_specs=[tile(), tile(ga_seg), tile(), tile(gm_seg), tile(gm_seg + 1), tile(), gate_spec,
                  const((D_MODEL, D_MODEL)), const((D_MODEL, D_MODEL)), const((1, D_MODEL)),
                  const((D_MODEL, D_MODEL)), const((1, D_MODEL))],
        out_specs=tile(),
        out_shape=jax.ShapeDtypeStruct((B, S, D_MODEL), F32),
        compiler_params=pltpu.CompilerParams(dimension_semantics=("arbitrary", "arbitrary"),
                                             vmem_limit_bytes=VMEM_LIMIT),
        name="post",
    )(o, ga, ca, gm, gm, x, gate, w_attn_o, w_conv_o, b_conv_o, w_out, final_g)


def kernel(x_prompt, x_sample, cache_k, cache_v, state_conv, page_table, c_prompt, c_sample, w_ada, b_ada, norm_g, w_in, lambda_q1, lambda_k1, lambda_q2, lambda_k2, subln_g, w_attn_o, conv_w, conv_b, conv_ln_g, conv_ln_b, w_conv_o, b_conv_o, w_out, final_g):
    assert w_ada.shape[0] == 1, "single-layer stack"
    B, S, _ = x_prompt.shape
    n_seq, t_new, _ = x_sample.shape
    n_tok = n_seq * t_new
    width = N_HEADS * DV

    row = lambda p: p.reshape(1, -1)
    w_in_b = w_in[0].astype(BF16)
    w_attn_o_b, w_conv_o_b, w_out_b = (w[0].astype(BF16) for w in (w_attn_o, w_conv_o, w_out))
    norm_g0, conv_b0, ln_g0, ln_b0 = row(norm_g[0]), row(conv_b[0]), row(conv_ln_g[0]), row(conv_ln_b[0])
    b_conv_o0, final_g0, subln_g0 = row(b_conv_o[0]), row(final_g), row(subln_g[0])
    lamp = jnp.stack([lambda_q1[0], lambda_k1[0], lambda_q2[0], lambda_k2[0]])

    n_c = B + n_seq
    pad = -n_c % 8
    c_all = jnp.concatenate([c_prompt, c_sample, jnp.zeros((pad, D_MODEL), F32)], axis=0)
    mod = _ada(c_all, w_ada[0], row(b_ada[0]))
    shift, scale, gate = jnp.split(mod[:n_c], 3, axis=-1)

    p_mod = lambda m: m[:B].reshape(B, 1, D_MODEL)
    q, k, kb, v, vb, ga, ca, gm, conv_prompt = _front(
        x_prompt, p_mod(scale), p_mod(shift), norm_g0, w_in_b, conv_w[0], conv_b0, ln_g0, ln_b0, tm=256)
    o = _attention(lamp, q, kb, vb, subln_g0, tq=256)
    y_prompt = _post(o, ga, 0, ca, gm, 0, x_prompt, p_mod(gate),
                     w_attn_o_b, w_conv_o_b, b_conv_o0, w_out_b, final_g0, tm=512)

    s_mod = lambda m: jnp.repeat(m[B:], t_new, axis=0)
    xs = x_sample.reshape(n_tok, D_MODEL)
    z = _sproj(xs, s_mod(scale), s_mod(shift), norm_g0, w_in_b)
    state = jnp.pad(state_conv[0], ((0, 0), (HALO - (CONV_W - 1), 0), (0, 0)))
    ca_s, conv_sample = _sconv(z, state, conv_w[0], conv_b0, ln_g0, ln_b0, n_seq, t_new)
    heads = lambda a, lead: a.reshape(1, *lead, N_HEADS, DV)
    k_sample = heads(z[:, width:2 * width], (n_seq, t_new))
    v_sample = heads(z[:, 2 * width:3 * width], (n_seq, t_new))
    n_phys = cache_k.shape[1]
    o_s = _decode(page_table, lamp, z, k_sample.reshape(-1, DV), v_sample.reshape(-1, DV),
                  cache_k[0].reshape(n_phys, PAGE * N_HEADS, DV), cache_v[0].reshape(n_phys, PAGE * N_HEADS, DV),
                  subln_g0, n_seq, t_new, pp=4)
    z3 = z.reshape(1, n_tok, D_IN)
    y_sample = _post(o_s.reshape(1, n_tok, width), z3, 3, ca_s.reshape(1, n_tok, D_MODEL), z3, 7,
                     xs.reshape(1, n_tok, D_MODEL), s_mod(gate).reshape(1, n_tok, D_MODEL),
                     w_attn_o_b, w_conv_o_b, b_conv_o0, w_out_b, final_g0, tm=n_tok)

    return (y_prompt, y_sample.reshape(n_seq, t_new, D_MODEL),
            heads(k, (B, S)), heads(v, (B, S)), conv_prompt[None],
            k_sample, v_sample, conv_sample[None])
```

```python
import functools
import math

import jax
import jax.numpy as jnp
from jax import lax
from jax.experimental import pallas as pl
from jax.experimental.pallas import tpu as pltpu

F32 = jnp.float32
BF16 = jnp.bfloat16

D_MODEL = 1024
N_HEADS = 8
DK = 64
DV = 2 * DK
CONV_W = 31
PAGE = 128
EPS = 1e-6
N_SEG = 9
D_IN = N_SEG * D_MODEL
LAM_INIT = 0.8 - 0.6 * math.exp(-0.3 * 0)
LOG2E = 1.4426950408889634
Q_SCALE = DK ** -0.5 * LOG2E

HALO = 32
VMEM_LIMIT = 56 * 1024 * 1024


def _silu(x):
    return x * jax.nn.sigmoid(x)


def _modulated_norm(x, g, scale, shift):
    ms = jnp.mean(x * x, axis=-1, keepdims=True)
    return (x * lax.rsqrt(ms + EPS) * g) * (1.0 + scale) + shift


def _layernorm_gate(y, ln_g, ln_b, gc):
    mu = jnp.mean(y, axis=-1, keepdims=True)
    yc = y - mu
    var = jnp.mean(yc * yc, axis=-1, keepdims=True)
    ln = yc * lax.rsqrt(var + EPS) * ln_g + ln_b
    return _silu(ln) * _silu(gc)


def _conv_rows(blk, cw, rc):
    base = HALO - (CONV_W - 1)
    acc = None
    n = rc + HALO
    for r in range(8):
        sh = blk if r == 0 else pltpu.roll(blk, n - r, axis=0)
        for a in range(HALO // 8 + 1):
            w = 8 * a + r - base
            if 0 <= w < CONV_W:
                term = sh[8 * a:8 * a + rc] * cw[w:w + 1]
                acc = term if acc is None else acc + term
    return acc


def _lam(lamp_ref):
    lp = lamp_ref[...]
    a = jnp.sum(lp[0:1] * lp[1:2], axis=-1, keepdims=True)
    b = jnp.sum(lp[2:3] * lp[3:4], axis=-1, keepdims=True)
    return jnp.exp(a) - jnp.exp(b) + LAM_INIT


def _subln(a, sg):
    ms = jnp.mean(a * a, axis=-1, keepdims=True)
    return a * lax.rsqrt(ms + EPS) * sg * (1.0 - LAM_INIT)


def _ada_kernel(c_ref, w_ref, b_ref, o_ref):
    a = _silu(c_ref[...]).astype(BF16)
    o_ref[...] = jnp.dot(a, w_ref[...].astype(BF16), preferred_element_type=F32) + b_ref[...]


def _ada(c_all, w_ada, b_ada):
    rows = c_all.shape[0]
    tn = 768
    return pl.pallas_call(
        _ada_kernel,
        grid=(3 * D_MODEL // tn,),
        in_specs=[pl.BlockSpec((rows, D_MODEL), lambda j: (0, 0)),
                  pl.BlockSpec((D_MODEL, tn), lambda j: (0, j)),
                  pl.BlockSpec((1, tn), lambda j: (0, j))],
        out_specs=pl.BlockSpec((rows, tn), lambda j: (0, j)),
        out_shape=jax.ShapeDtypeStruct((rows, 3 * D_MODEL), F32),
        name="ada",
    )(c_all, w_ada, b_ada)


def _front_kernel(x_ref, scale_ref, shift_ref, g_ref, w_ref, cw_ref, cb_ref, lng_ref, lnb_ref,
                  q_ref, k_ref, kb_ref, v_ref, vb_ref, ga_ref, ca_ref, gm_ref, cst_ref,
                  h_scr, uext_scr, y_scr, *, tm):
    t = pl.program_id(1)
    h_scr[...] = _modulated_norm(x_ref[0], g_ref[...], scale_ref[0], shift_ref[0]).astype(BF16)

    def seg(j, n=1):
        return jnp.dot(h_scr[...], w_ref[:, j * D_MODEL:(j + n) * D_MODEL], preferred_element_type=F32)

    q_ref[0] = (seg(0) * Q_SCALE).astype(BF16)
    zk = seg(1)
    k_ref[0] = zk
    kb_ref[0] = zk.astype(BF16)
    zv = seg(2)
    v_ref[0] = zv
    vb_ref[0] = zv.astype(BF16)
    ga_ref[0] = seg(3).astype(BF16)
    gm_ref[0] = seg(7, 2).astype(BF16)

    @pl.when(t == 0)
    def _():
        uext_scr[0:HALO, :] = jnp.zeros((HALO, D_MODEL), F32)

    uext_scr[HALO:HALO + tm, :] = seg(4) * jax.nn.sigmoid(seg(5))

    rc = 64

    def conv_rows(ci, carry):
        r0 = pl.multiple_of(ci * rc, rc)
        for lc in range(D_MODEL // 128):
            ls = slice(lc * 128, (lc + 1) * 128)
            y_scr[pl.ds(r0, rc), ls] = _conv_rows(uext_scr[pl.ds(r0, rc + HALO), ls], cw_ref[:, ls], rc)
        return carry

    lax.fori_loop(0, tm // rc, conv_rows, 0)

    last = uext_scr[tm:tm + HALO, :]
    uext_scr[0:HALO, :] = last

    @pl.when(t == pl.num_programs(1) - 1)
    def _():
        cst_ref[0] = last[HALO - (CONV_W - 1):HALO]

    ca_ref[0] = _layernorm_gate(y_scr[...] + cb_ref[...], lng_ref[...], lnb_ref[...], seg(6)).astype(BF16)


def _front(x, scale, shift, norm_g, w_in, conv_w, conv_b, ln_g, ln_b, tm):
    B, S, _ = x.shape
    tile = lambda width: pl.BlockSpec((1, tm, width), lambda b, t: (b, t, 0))
    row = lambda: pl.BlockSpec((1, 1, D_MODEL), lambda b, t: (b, 0, 0))
    const = lambda shape: pl.BlockSpec(shape, lambda b, t: (0,) * len(shape))
    bsd = lambda width, dt: jax.ShapeDtypeStruct((B, S, width), dt)
    return pl.pallas_call(
        functools.partial(_front_kernel, tm=tm),
        grid=(B, S // tm),
        in_specs=[tile(D_MODEL), row(), row(), const((1, D_MODEL)),
                  pl.BlockSpec((D_MODEL, D_IN), lambda b, t: (0, 0), pipeline_mode=pl.Buffered(1)),
                  const((CONV_W, D_MODEL)), const((1, D_MODEL)), const((1, D_MODEL)), const((1, D_MODEL))],
        out_specs=[tile(D_MODEL), tile(D_MODEL), tile(D_MODEL), tile(D_MODEL), tile(D_MODEL),
                   tile(D_MODEL), tile(D_MODEL), tile(2 * D_MODEL),
                   pl.BlockSpec((1, CONV_W - 1, D_MODEL), lambda b, t: (b, 0, 0))],
        out_shape=[bsd(D_MODEL, BF16), bsd(D_MODEL, F32), bsd(D_MODEL, BF16), bsd(D_MODEL, F32),
                   bsd(D_MODEL, BF16), bsd(D_MODEL, BF16), bsd(D_MODEL, BF16), bsd(2 * D_MODEL, BF16),
                   jax.ShapeDtypeStruct((B, CONV_W - 1, D_MODEL), F32)],
        scratch_shapes=[pltpu.VMEM((tm, D_MODEL), BF16),
                        pltpu.VMEM((HALO + tm, D_MODEL), F32),
                        pltpu.VMEM((tm, D_MODEL), F32)],
        compiler_params=pltpu.CompilerParams(dimension_semantics=("arbitrary", "arbitrary"),
                                             vmem_limit_bytes=VMEM_LIMIT),
        name="front",
    )(x, scale, shift, norm_g, w_in, conv_w, conv_b, ln_g, ln_b)


def _attn_kernel(lamp_ref, q_ref, k_ref, v_ref, sg_ref, o_ref, qbd_scr, vt_scr, acc_scr, *, tq, tk):
    i = pl.program_id(2)
    n_kv = v_ref.shape[1] // tk
    n_diag = tq // tk
    n_full = i * n_diag

    @pl.when(i == 0)
    def _():
        def transpose_v(c, carry):
            c0 = pl.multiple_of(c * tk, tk)
            vt_scr[c] = v_ref[0, pl.ds(c0, tk), :].astype(F32).T.astype(BF16)
            return carry
        lax.fori_loop(0, n_kv, transpose_v, 0)

    qt = q_ref[0].astype(F32).T
    depth = lax.broadcasted_iota(jnp.int32, (DV, tq), 0)
    qbd_scr[...] = jnp.concatenate([jnp.where(depth < DK, qt, 0.0), jnp.where(depth >= DK, qt, 0.0)],
                                   axis=1).astype(BF16)
    acc_scr[...] = jnp.zeros((DV, 2 * tq), F32)

    def scores(j):
        k0 = pl.multiple_of(j * tk, tk)
        return jnp.dot(k_ref[0, pl.ds(k0, tk), :], qbd_scr[...], preferred_element_type=F32)

    def update(s, j, m_prev, l_prev):
        m_new = jnp.maximum(m_prev, jnp.max(s, axis=0, keepdims=True))
        alpha = jnp.exp2(m_prev - m_new)
        p = jnp.exp2(s - m_new)
        acc_scr[...] = alpha * acc_scr[...] + jnp.dot(vt_scr[j], p.astype(BF16), preferred_element_type=F32)
        return m_new, alpha * l_prev + jnp.sum(p, axis=0, keepdims=True)

    def body(jj, carry):
        s, m, l = carry
        for j in (2 * jj, 2 * jj + 1):
            s_next = scores(j + 1)
            m, l = update(s, j, m, l)
            s = s_next
        return s, m, l

    init = (scores(0), jnp.full((1, 2 * tq), -jnp.inf, F32), jnp.zeros((1, 2 * tq), F32))
    s, m, l = lax.fori_loop(0, i * (n_diag // 2), body, init)
    key = lax.broadcasted_iota(jnp.int32, (tk, 2 * tq), 0)
    qry = lax.broadcasted_iota(jnp.int32, (tk, 2 * tq), 1)
    qry = jnp.where(qry >= tq, qry - tq, qry)
    for d in range(n_diag):
        s_next = scores(n_full + d + 1) if d + 1 < n_diag else None
        m, l = update(jnp.where(key + d * tk <= qry, s, -jnp.inf), n_full + d, m, l)
        s = s_next

    ot = acc_scr[...] / l
    a = ot[:, 0:tq] - _lam(lamp_ref) * ot[:, tq:2 * tq]
    ms = jnp.mean(a * a, axis=0, keepdims=True)
    o_ref[0] = ((a * lax.rsqrt(ms + EPS)).T * sg_ref[...] * (1.0 - LAM_INIT)).astype(BF16)


def _attention(lamp, q, kb, vb, subln_g, tq, tk):
    B, S, _ = q.shape
    assert tq % tk == 0 and S % tq == 0
    return pl.pallas_call(
        functools.partial(_attn_kernel, tq=tq, tk=tk),
        grid=(B, N_HEADS, S // tq),
        in_specs=[pl.BlockSpec((4, DK), lambda b, h, i: (0, 0)),
                  pl.BlockSpec((1, tq, DV), lambda b, h, i: (b, i, h)),
                  pl.BlockSpec((1, S, DV), lambda b, h, i: (b, 0, h)),
                  pl.BlockSpec((1, S, DV), lambda b, h, i: (b, 0, h)),
                  pl.BlockSpec((1, DV), lambda b, h, i: (0, 0))],
        out_specs=pl.BlockSpec((1, tq, DV), lambda b, h, i: (b, i, h)),
        out_shape=jax.ShapeDtypeStruct((B, S, N_HEADS * DV), BF16),
        scratch_shapes=[pltpu.VMEM((DV, 2 * tq), BF16),
                        pltpu.VMEM((S // tk, DV, tk), BF16),
                        pltpu.VMEM((DV, 2 * tq), F32)],
        compiler_params=pltpu.CompilerParams(dimension_semantics=("arbitrary", "arbitrary", "arbitrary"),
                                             vmem_limit_bytes=VMEM_LIMIT),
        name="attn",
    )(lamp, q, kb, vb, subln_g)


def _sproj_kernel(x_ref, scale_ref, shift_ref, g_ref, w_ref, z_ref, h_scr):
    @pl.when(pl.program_id(0) == 0)
    def _():
        h_scr[...] = _modulated_norm(x_ref[...], g_ref[...], scale_ref[...], shift_ref[...]).astype(BF16)

    z_ref[...] = jnp.dot(h_scr[...], w_ref[...], preferred_element_type=F32)


def _sproj(x, scale, shift, norm_g, w_in):
    n = x.shape[0]
    full = pl.BlockSpec((n, D_MODEL), lambda j: (0, 0))
    return pl.pallas_call(
        _sproj_kernel,
        grid=(N_SEG,),
        in_specs=[full, full, full, pl.BlockSpec((1, D_MODEL), lambda j: (0, 0)),
                  pl.BlockSpec((D_MODEL, D_MODEL), lambda j: (0, j))],
        out_specs=pl.BlockSpec((n, D_MODEL), lambda j: (0, j)),
        out_shape=jax.ShapeDtypeStruct((n, D_IN), F32),
        scratch_shapes=[pltpu.VMEM((n, D_MODEL), BF16)],
        compiler_params=pltpu.CompilerParams(dimension_semantics=("arbitrary",)),
        name="sproj",
    )(x, scale, shift, norm_g, w_in)


def _sconv_kernel(za_ref, zb_ref, zc_ref, st_ref, cw_ref, cb_ref, lng_ref, lnb_ref,
                  ca_ref, cst_ref, u_scr, uext_scr, y_scr, *, n_seq, t_new):
    u_scr[...] = za_ref[...] * jax.nn.sigmoid(zb_ref[...])

    def one_seq(b, carry):
        r0 = pl.multiple_of(b * t_new, t_new)
        uext_scr[0:HALO, :] = st_ref[b]
        uext_scr[HALO:HALO + t_new, :] = u_scr[pl.ds(r0, t_new), :]
        blk = uext_scr[...]
        y_scr[pl.ds(r0, t_new), :] = _conv_rows(blk, cw_ref[...], t_new)
        cst_ref[b] = blk[HALO + t_new - (CONV_W - 1):HALO + t_new]
        return carry

    lax.fori_loop(0, n_seq, one_seq, 0)
    ca_ref[...] = _layernorm_gate(y_scr[...] + cb_ref[...], lng_ref[...], lnb_ref[...],
                                  zc_ref[...]).astype(BF16)


def _sconv(z, state, conv_w, conv_b, ln_g, ln_b, n_seq, t_new):
    n = n_seq * t_new
    zseg = lambda j: pl.BlockSpec((n, D_MODEL), lambda i: (0, j))
    const = lambda shape: pl.BlockSpec(shape, lambda i: (0,) * len(shape))
    return pl.pallas_call(
        functools.partial(_sconv_kernel, n_seq=n_seq, t_new=t_new),
        grid=(1,),
        in_specs=[zseg(4), zseg(5), zseg(6), const((n_seq, HALO, D_MODEL)),
                  const((CONV_W, D_MODEL)), const((1, D_MODEL)), const((1, D_MODEL)), const((1, D_MODEL))],
        out_specs=[const((n, D_MODEL)), const((n_seq, CONV_W - 1, D_MODEL))],
        out_shape=[jax.ShapeDtypeStruct((n, D_MODEL), BF16),
                   jax.ShapeDtypeStruct((n_seq, CONV_W - 1, D_MODEL), F32)],
        scratch_shapes=[pltpu.VMEM((n, D_MODEL), F32),
                        pltpu.VMEM((HALO + t_new, D_MODEL), F32),
                        pltpu.VMEM((n, D_MODEL), F32)],
        name="sconv",
    )(z, z, z, state, conv_w, conv_b, ln_g, ln_b)


def _decode_kernel(pt_ref, lamp_ref, q_ref, kn_ref, vn_ref, sg_ref, *rest, pp, t_new):
    del pt_ref
    k_refs, v_refs = rest[:pp], rest[pp:2 * pp]
    o_ref, qrow_scr, bias_scr, m_scr, l_scr, acc_scr = rest[2 * pp:]
    step = pl.program_id(1)
    n_rows = N_HEADS * 2 * t_new
    page_rows = PAGE * N_HEADS
    new_rows = t_new * N_HEADS
    log2_heads = N_HEADS.bit_length() - 1
    log2_pair = (2 * t_new).bit_length() - 1
    contract_last = (((1,), (1,)), ((), ()))

    def online_update(s, v, first):
        s_max = jnp.max(s, axis=1, keepdims=True)
        if first:
            m_new = s_max
            p = jnp.exp2(s - m_new)
            l_scr[...] = jnp.sum(p, axis=1, keepdims=True)
            acc_scr[...] = jnp.dot(p.astype(BF16), v, preferred_element_type=F32)
        else:
            m_prev = m_scr[...]
            m_new = jnp.maximum(m_prev, s_max)
            alpha = jnp.exp2(m_prev - m_new)
            p = jnp.exp2(s - m_new)
            l_scr[...] = alpha * l_scr[...] + jnp.sum(p, axis=1, keepdims=True)
            acc_scr[...] = alpha * acc_scr[...] + jnp.dot(p.astype(BF16), v, preferred_element_type=F32)
        m_scr[...] = m_new

    @pl.when(step == 0)
    def _():
        q = q_ref[...] * Q_SCALE
        lane = lax.broadcasted_iota(jnp.int32, (t_new, DV), 1)
        rows = []
        for h in range(N_HEADS):
            qh = q[:, h * DV:(h + 1) * DV]
            rows += [jnp.where(lane < DK, qh, 0.0), jnp.where(lane >= DK, qh, 0.0)]
        qrow_scr[...] = jnp.concatenate(rows, axis=0).astype(BF16)

        row = lax.broadcasted_iota(jnp.int32, (n_rows, page_rows), 0)
        col = lax.broadcasted_iota(jnp.int32, (n_rows, page_rows), 1)
        same_head = (col & (N_HEADS - 1)) == (row >> log2_pair)
        bias_scr[...] = jnp.where(same_head, 0.0, -jnp.inf)

        pad = jnp.zeros((PAGE - new_rows, DV), F32)
        kn = jnp.concatenate([kn_ref[...], pad], axis=0).astype(BF16)
        vn = jnp.concatenate([vn_ref[...], pad], axis=0).astype(BF16)
        s = lax.dot_general(qrow_scr[...], kn, contract_last, preferred_element_type=F32)
        row = lax.broadcasted_iota(jnp.int32, (n_rows, PAGE), 0)
        col = lax.broadcasted_iota(jnp.int32, (n_rows, PAGE), 1)
        visible = ((col & (N_HEADS - 1)) == (row >> log2_pair)) & ((col >> log2_heads) <= (row & (t_new - 1)))
        online_update(jnp.where(visible, s, -jnp.inf), vn, True)

    q_rows = qrow_scr[...]
    bias = bias_scr[...]
    s = jnp.concatenate(
        [lax.dot_general(q_rows, r[0].astype(BF16), contract_last, preferred_element_type=F32) + bias
         for r in k_refs], axis=1)
    v = jnp.concatenate([r[0] for r in v_refs], axis=0).astype(BF16)
    online_update(s, v, False)

    @pl.when(step == pl.num_programs(1) - 1)
    def _():
        lam = _lam(lamp_ref)
        o = acc_scr[...] / l_scr[...]
        for h in range(N_HEADS):
            r = h * 2 * t_new
            a = o[r:r + t_new] - lam * o[r + t_new:r + 2 * t_new]
            o_ref[:, h * DV:(h + 1) * DV] = _subln(a, sg_ref[...])


def _decode(page_table, lamp, q, k_new, v_new, cache_k, cache_v, subln_g, n_seq, t_new, pp):
    n_pages = page_table.shape[1]
    width = N_HEADS * DV
    page_rows = PAGE * N_HEADS
    new_rows = t_new * N_HEADS
    n_rows = N_HEADS * 2 * t_new
    assert t_new & (t_new - 1) == 0 and N_HEADS & (N_HEADS - 1) == 0
    assert n_pages % pp == 0 and new_rows <= PAGE
    page = lambda i: pl.BlockSpec((1, page_rows, DV), lambda b, s, pt: (pt[b, s * pp + i], 0, 0))
    new = lambda: pl.BlockSpec((new_rows, DV), lambda b, s, pt: (b, 0))
    grid_spec = pltpu.PrefetchScalarGridSpec(
        num_scalar_prefetch=1,
        grid=(n_seq, n_pages // pp),
        in_specs=[pl.BlockSpec((4, DK), lambda b, s, pt: (0, 0)),
                  pl.BlockSpec((t_new, width), lambda b, s, pt: (b, 0)), new(), new(),
                  pl.BlockSpec((1, DV), lambda b, s, pt: (0, 0))]
                 + [page(i) for i in range(pp)] + [page(i) for i in range(pp)],
        out_specs=pl.BlockSpec((t_new, width), lambda b, s, pt: (b, 0)),
        scratch_shapes=[pltpu.VMEM((n_rows, DV), BF16),
                        pltpu.VMEM((n_rows, page_rows), F32),
                        pltpu.VMEM((n_rows, 1), F32),
                        pltpu.VMEM((n_rows, 1), F32),
                        pltpu.VMEM((n_rows, DV), F32)])
    return pl.pallas_call(
        functools.partial(_decode_kernel, pp=pp, t_new=t_new),
        grid_spec=grid_spec,
        out_shape=jax.ShapeDtypeStruct((n_seq * t_new, width), F32),
        compiler_params=pltpu.CompilerParams(dimension_semantics=("arbitrary", "arbitrary"),
                                             vmem_limit_bytes=VMEM_LIMIT),
        name="decode",
    )(page_table, lamp, q, k_new, v_new, subln_g, *([cache_k] * pp), *([cache_v] * pp))


def _post_kernel(o_ref, ga_ref, ca_ref, gma_ref, gmc_ref, x_ref, gate_ref,
                 wa_ref, wc_ref, bc_ref, wo_ref, fg_ref, y_ref):
    a = (o_ref[0].astype(F32) * _silu(ga_ref[0].astype(F32))).astype(BF16)
    ya = jnp.dot(a, wa_ref[...], preferred_element_type=F32)
    yc = jnp.dot(ca_ref[0], wc_ref[...], preferred_element_type=F32) + bc_ref[...]
    m = (jax.nn.sigmoid(gma_ref[0].astype(F32)) * ya + jax.nn.sigmoid(gmc_ref[0].astype(F32)) * yc)
    d = jnp.dot(m.astype(BF16), wo_ref[...], preferred_element_type=F32)
    xo = x_ref[0] + gate_ref[0] * d
    ms = jnp.mean(xo * xo, axis=-1, keepdims=True)
    y_ref[0] = xo * lax.rsqrt(ms + EPS) * fg_ref[...]


def _post(o, ga, ga_seg, ca, gm, gm_seg, x, gate, w_attn_o, w_conv_o, b_conv_o, w_out, final_g, tm):
    B, S, _ = x.shape
    gate_rows = gate.shape[1]
    tile = lambda seg=0: pl.BlockSpec((1, tm, D_MODEL), lambda b, t: (b, t, seg))
    const = lambda shape: pl.BlockSpec(shape, lambda b, t: (0,) * len(shape))
    if gate_rows == 1:
        gate_spec = pl.BlockSpec((1, 1, D_MODEL), lambda b, t: (b, 0, 0))
    else:
        gate_spec = tile()
    return pl.pallas_call(
        _post_kernel,
        grid=(B, S // tm),
        in_specs=[tile(), tile(ga_seg), tile(), tile(gm_seg), tile(gm_seg + 1), tile(), gate_spec,
                  const((D_MODEL, D_MODEL)), const((D_MODEL, D_MODEL)), const((1, D_MODEL)),
                  const((D_MODEL, D_MODEL)), const((1, D_MODEL))],
        out_specs=tile(),
        out_shape=jax.ShapeDtypeStruct((B, S, D_MODEL), F32),
        compiler_params=pltpu.CompilerParams(dimension_semantics=("arbitrary", "arbitrary"),
                                             vmem_limit_bytes=VMEM_LIMIT),
        name="post",
    )(o, ga, ca, gm, gm, x, gate, w_attn_o, w_conv_o, b_conv_o, w_out, final_g)


def kernel(x_prompt, x_sample, cache_k, cache_v, state_conv, page_table, c_prompt, c_sample, w_ada, b_ada, norm_g, w_in, lambda_q1, lambda_k1, lambda_q2, lambda_k2, subln_g, w_attn_o, conv_w, conv_b, conv_ln_g, conv_ln_b, w_conv_o, b_conv_o, w_out, final_g):
    assert w_ada.shape[0] == 1, "single-layer stack"
    B, S, _ = x_prompt.shape
    n_seq, t_new, _ = x_sample.shape
    n_tok = n_seq * t_new
    width = N_HEADS * DV

    row = lambda p: p.reshape(1, -1)
    w_in_b = w_in[0].astype(BF16)
    w_attn_o_b, w_conv_o_b, w_out_b = (w[0].astype(BF16) for w in (w_attn_o, w_conv_o, w_out))
    norm_g0, conv_b0, ln_g0, ln_b0 = row(norm_g[0]), row(conv_b[0]), row(conv_ln_g[0]), row(conv_ln_b[0])
    b_conv_o0, final_g0, subln_g0 = row(b_conv_o[0]), row(final_g), row(subln_g[0])
    lamp = jnp.stack([lambda_q1[0], lambda_k1[0], lambda_q2[0], lambda_k2[0]])

    n_c = B + n_seq
    pad = -n_c % 8
    c_all = jnp.concatenate([c_prompt, c_sample, jnp.zeros((pad, D_MODEL), F32)], axis=0)
    mod = _ada(c_all, w_ada[0], row(b_ada[0]))
    shift, scale, gate = jnp.split(mod[:n_c], 3, axis=-1)

    p_mod = lambda m: m[:B].reshape(B, 1, D_MODEL)
    q, k, kb, v, vb, ga, ca, gm, conv_prompt = _front(
        x_prompt, p_mod(scale), p_mod(shift), norm_g0, w_in_b, conv_w[0], conv_b0, ln_g0, ln_b0, tm=256)
    o = _attention(lamp, q, kb, vb, subln_g0, tq=512, tk=256)
    y_prompt = _post(o, ga, 0, ca, gm, 0, x_prompt, p_mod(gate),
                     w_attn_o_b, w_conv_o_b, b_conv_o0, w_out_b, final_g0, tm=512)

    s_mod = lambda m: jnp.repeat(m[B:], t_new, axis=0)
    xs = x_sample.reshape(n_tok, D_MODEL)
    z = _sproj(xs, s_mod(scale), s_mod(shift), norm_g0, w_in_b)
    state = jnp.pad(state_conv[0], ((0, 0), (HALO - (CONV_W - 1), 0), (0, 0)))
    ca_s, conv_sample = _sconv(z, state, conv_w[0], conv_b0, ln_g0, ln_b0, n_seq, t_new)
    heads = lambda a, lead: a.reshape(1, *lead, N_HEADS, DV)
    k_sample = heads(z[:, width:2 * width], (n_seq, t_new))
    v_sample = heads(z[:, 2 * width:3 * width], (n_seq, t_new))
    n_phys = cache_k.shape[1]
    o_s = _decode(page_table, lamp, z, k_sample.reshape(-1, DV), v_sample.reshape(-1, DV),
                  cache_k[0].reshape(n_phys, PAGE * N_HEADS, DV), cache_v[0].reshape(n_phys, PAGE * N_HEADS, DV),
                  subln_g0, n_seq, t_new, pp=8)
    z3 = z.reshape(1, n_tok, D_IN)
    y_sample = _post(o_s.reshape(1, n_tok, width), z3, 3, ca_s.reshape(1, n_tok, D_MODEL), z3, 7,
                     xs.reshape(1, n_tok, D_MODEL), s_mod(gate).reshape(1, n_tok, D_MODEL),
                     w_attn_o_b, w_conv_o_b, b_conv_o0, w_out_b, final_g0, tm=n_tok)

    return (y_prompt, y_sample.reshape(n_seq, t_new, D_MODEL),
            heads(k, (B, S)), heads(v, (B, S)), conv_prompt[None],
            k_sample, v_sample, conv_sample[None])
```

```python
import functools
import math

import jax
import jax.numpy as jnp
from jax import lax
from jax.experimental import pallas as pl
from jax.experimental.pallas import tpu as pltpu

F32 = jnp.float32
BF16 = jnp.bfloat16

D_MODEL = 1024
N_HEADS = 8
DK = 64
DV = 2 * DK
CONV_W = 31
PAGE = 128
EPS = 1e-6
N_SEG = 9
D_IN = N_SEG * D_MODEL
LAM_INIT = 0.8 - 0.6 * math.exp(-0.3 * 0)
LOG2E = 1.4426950408889634
Q_SCALE = DK ** -0.5 * LOG2E

ONES_ROWS = 16
HALO = 32
VMEM_LIMIT = 56 * 1024 * 1024


def _silu(x):
    return x * jax.nn.sigmoid(x)


def _modulated_norm(x, g, scale, shift):
    ms = jnp.mean(x * x, axis=-1, keepdims=True)
    return (x * lax.rsqrt(ms + EPS) * g) * (1.0 + scale) + shift


def _layernorm_gate(y, ln_g, ln_b, gc):
    mu = jnp.mean(y, axis=-1, keepdims=True)
    yc = y - mu
    var = jnp.mean(yc * yc, axis=-1, keepdims=True)
    ln = yc * lax.rsqrt(var + EPS) * ln_g + ln_b
    return _silu(ln) * _silu(gc)


def _conv_rows(blk, cw, rc):
    base = HALO - (CONV_W - 1)
    acc = None
    n = rc + HALO
    for r in range(8):
        sh = blk if r == 0 else pltpu.roll(blk, n - r, axis=0)
        for a in range(HALO // 8 + 1):
            w = 8 * a + r - base
            if 0 <= w < CONV_W:
                term = sh[8 * a:8 * a + rc] * cw[w:w + 1]
                acc = term if acc is None else acc + term
    return acc


def _lam(lamp_ref):
    lp = lamp_ref[...]
    a = jnp.sum(lp[0:1] * lp[1:2], axis=-1, keepdims=True)
    b = jnp.sum(lp[2:3] * lp[3:4], axis=-1, keepdims=True)
    return jnp.exp(a) - jnp.exp(b) + LAM_INIT


def _subln(a, sg):
    ms = jnp.mean(a * a, axis=-1, keepdims=True)
    return a * lax.rsqrt(ms + EPS) * sg * (1.0 - LAM_INIT)


def _ada_kernel(c_ref, w_ref, b_ref, o_ref):
    a = _silu(c_ref[...]).astype(BF16)
    o_ref[...] = jnp.dot(a, w_ref[...].astype(BF16), preferred_element_type=F32) + b_ref[...]


def _ada(c_all, w_ada, b_ada):
    rows = c_all.shape[0]
    tn = 768
    return pl.pallas_call(
        _ada_kernel,
        grid=(3 * D_MODEL // tn,),
        in_specs=[pl.BlockSpec((rows, D_MODEL), lambda j: (0, 0)),
                  pl.BlockSpec((D_MODEL, tn), lambda j: (0, j)),
                  pl.BlockSpec((1, tn), lambda j: (0, j))],
        out_specs=pl.BlockSpec((rows, tn), lambda j: (0, j)),
        out_shape=jax.ShapeDtypeStruct((rows, 3 * D_MODEL), F32),
        name="ada",
    )(c_all, w_ada, b_ada)


def _front_kernel(x_ref, scale_ref, shift_ref, g_ref, w_ref, cw_ref, cb_ref, lng_ref, lnb_ref,
                  q_ref, k_ref, kb_ref, v_ref, vb_ref, ga_ref, ca_ref, gm_ref, cst_ref,
                  h_scr, uext_scr, y_scr, *, tm):
    t = pl.program_id(1)
    h_scr[...] = _modulated_norm(x_ref[0], g_ref[...], scale_ref[0], shift_ref[0]).astype(BF16)

    def seg(j, n=1):
        return jnp.dot(h_scr[...], w_ref[:, j * D_MODEL:(j + n) * D_MODEL], preferred_element_type=F32)

    @pl.when(t == 0)
    def _():
        uext_scr[0:HALO, :] = jnp.zeros((HALO, D_MODEL), F32)

    uext_scr[HALO:HALO + tm, :] = seg(4) * jax.nn.sigmoid(seg(5))

    rc = 64

    def conv_rows(ci, carry):
        r0 = pl.multiple_of(ci * rc, rc)
        for lc in range(D_MODEL // 128):
            ls = slice(lc * 128, (lc + 1) * 128)
            y_scr[pl.ds(r0, rc), ls] = _conv_rows(uext_scr[pl.ds(r0, rc + HALO), ls], cw_ref[:, ls], rc)
        return carry

    lax.fori_loop(0, tm // rc, conv_rows, 0)

    q_ref[0] = (seg(0) * Q_SCALE).astype(BF16)
    zk = seg(1)
    k_ref[0] = zk
    kb_ref[0] = zk.astype(BF16)
    zv = seg(2)
    v_ref[0] = zv
    vb_ref[0] = zv.astype(BF16)
    ga_ref[0] = seg(3).astype(BF16)
    gm_ref[0] = seg(7, 2).astype(BF16)
    ca_ref[0] = _layernorm_gate(y_scr[...] + cb_ref[...], lng_ref[...], lnb_ref[...], seg(6)).astype(BF16)

    last = uext_scr[tm:tm + HALO, :]
    uext_scr[0:HALO, :] = last

    @pl.when(t == pl.num_programs(1) - 1)
    def _():
        cst_ref[0] = last[HALO - (CONV_W - 1):HALO]


def _front(x, scale, shift, norm_g, w_in, conv_w, conv_b, ln_g, ln_b, tm):
    B, S, _ = x.shape
    tile = lambda width: pl.BlockSpec((1, tm, width), lambda b, t: (b, t, 0))
    row = lambda: pl.BlockSpec((1, 1, D_MODEL), lambda b, t: (b, 0, 0))
    const = lambda shape: pl.BlockSpec(shape, lambda b, t: (0,) * len(shape))
    bsd = lambda width, dt: jax.ShapeDtypeStruct((B, S, width), dt)
    return pl.pallas_call(
        functools.partial(_front_kernel, tm=tm),
        grid=(B, S // tm),
        in_specs=[tile(D_MODEL), row(), row(), const((1, D_MODEL)),
                  pl.BlockSpec((D_MODEL, D_IN), lambda b, t: (0, 0), pipeline_mode=pl.Buffered(1)),
                  const((CONV_W, D_MODEL)), const((1, D_MODEL)), const((1, D_MODEL)), const((1, D_MODEL))],
        out_specs=[tile(D_MODEL), tile(D_MODEL), tile(D_MODEL), tile(D_MODEL), tile(D_MODEL),
                   tile(D_MODEL), tile(D_MODEL), tile(2 * D_MODEL),
                   pl.BlockSpec((1, CONV_W - 1, D_MODEL), lambda b, t: (b, 0, 0))],
        out_shape=[bsd(D_MODEL, BF16), bsd(D_MODEL, F32), bsd(D_MODEL, BF16), bsd(D_MODEL, F32),
                   bsd(D_MODEL, BF16), bsd(D_MODEL, BF16), bsd(D_MODEL, BF16), bsd(2 * D_MODEL, BF16),
                   jax.ShapeDtypeStruct((B, CONV_W - 1, D_MODEL), F32)],
        scratch_shapes=[pltpu.VMEM((tm, D_MODEL), BF16),
                        pltpu.VMEM((HALO + tm, D_MODEL), F32),
                        pltpu.VMEM((tm, D_MODEL), F32)],
        compiler_params=pltpu.CompilerParams(dimension_semantics=("arbitrary", "arbitrary"),
                                             vmem_limit_bytes=VMEM_LIMIT),
        name="front",
    )(x, scale, shift, norm_g, w_in, conv_w, conv_b, ln_g, ln_b)


def _attn_kernel(lamp_ref, q_ref, k_ref, v_ref, sg_ref, o_ref, qbd_scr, vt_scr, acc_scr, m_scr, *s_scrs, tq, tk, unroll):
    i = pl.program_id(2)
    n_kv = v_ref.shape[1] // tk
    n_diag = tq // tk
    n_full = i * n_diag
    UNROLL = unroll

    @pl.when(i == 0)
    def _():
        extra = lax.broadcasted_iota(jnp.int32, (ONES_ROWS, tk), 0)
        ones_rows = jnp.where(extra == 0, 1.0, 0.0)

        def transpose_v(c, carry):
            c0 = pl.multiple_of(c * tk, tk)
            vt = v_ref[0, pl.ds(c0, tk), :].astype(F32).T
            vt_scr[c] = jnp.concatenate([vt, ones_rows], axis=0).astype(BF16)
            return carry
        lax.fori_loop(0, n_kv, transpose_v, 0)

    qt = q_ref[0].astype(F32).T
    depth = lax.broadcasted_iota(jnp.int32, (DV, tq), 0)
    qbd_scr[...] = jnp.concatenate([jnp.where(depth < DK, qt, 0.0), jnp.where(depth >= DK, qt, 0.0)],
                                   axis=1).astype(BF16)
    acc_scr[...] = jnp.zeros((DV + ONES_ROWS, 2 * tq), F32)

    def scores(j):
        k0 = pl.multiple_of(j * tk, tk)
        return jnp.dot(k_ref[0, pl.ds(k0, tk), :], qbd_scr[...], preferred_element_type=F32)

    def update(s, j):
        m_prev = m_scr[...]
        m_new = jnp.maximum(m_prev, jnp.max(s, axis=0, keepdims=True))
        alpha = jnp.exp2(m_prev - m_new)
        p = jnp.exp2(s - m_new).astype(BF16)
        acc_scr[...] = alpha * acc_scr[...] + jnp.dot(vt_scr[j], p, preferred_element_type=F32)
        m_scr[...] = m_new

    def past_tiles(j0, count):
        for t in range(count):
            s_scrs[1 - t % 2][...] = scores(j0 + t + 1)
            update(s_scrs[t % 2][...], j0 + t)

    m_scr[...] = jnp.full((1, 2 * tq), -jnp.inf, F32)
    s_scrs[0][...] = scores(0)

    def body(jj, carry):
        past_tiles(UNROLL * jj, UNROLL)
        return carry

    lax.fori_loop(0, n_full // UNROLL, body, 0)

    if UNROLL > 2:
        @pl.when(n_full % UNROLL != 0)
        def _():
            past_tiles(n_full - 2, 2)

    key = lax.broadcasted_iota(jnp.int32, (tk, 2 * tq), 0)
    qry = lax.broadcasted_iota(jnp.int32, (tk, 2 * tq), 1)
    qry = jnp.where(qry >= tq, qry - tq, qry)
    for d in range(n_diag):
        if d + 1 < n_diag:
            s_scrs[(d + 1) % 2][...] = scores(n_full + d + 1)
        update(jnp.where(key + d * tk <= qry, s_scrs[d % 2][...], -jnp.inf), n_full + d)

    ot = acc_scr[0:DV, :] / acc_scr[DV:DV + 1, :]
    a = ot[:, 0:tq] - _lam(lamp_ref) * ot[:, tq:2 * tq]
    ms = jnp.mean(a * a, axis=0, keepdims=True)
    o_ref[0] = ((a * lax.rsqrt(ms + EPS)).T * sg_ref[...] * (1.0 - LAM_INIT)).astype(BF16)


def _attention(lamp, q, kb, vb, subln_g, tq, tk, unroll):
    B, S, _ = q.shape
    assert tq == 2 * tk and S % tq == 0 and unroll in (2, 4)
    return pl.pallas_call(
        functools.partial(_attn_kernel, tq=tq, tk=tk, unroll=unroll),
        grid=(B, N_HEADS, S // tq),
        in_specs=[pl.BlockSpec((4, DK), lambda b, h, i: (0, 0)),
                  pl.BlockSpec((1, tq, DV), lambda b, h, i: (b, i, h)),
                  pl.BlockSpec((1, S, DV), lambda b, h, i: (b, 0, h)),
                  pl.BlockSpec((1, S, DV), lambda b, h, i: (b, 0, h)),
                  pl.BlockSpec((1, DV), lambda b, h, i: (0, 0))],
        out_specs=pl.BlockSpec((1, tq, DV), lambda b, h, i: (b, i, h)),
        out_shape=jax.ShapeDtypeStruct((B, S, N_HEADS * DV), BF16),
        scratch_shapes=[pltpu.VMEM((DV, 2 * tq), BF16),
                        pltpu.VMEM((S // tk, DV + ONES_ROWS, tk), BF16),
                        pltpu.VMEM((DV + ONES_ROWS, 2 * tq), F32),
                        pltpu.VMEM((1, 2 * tq), F32),
                        pltpu.VMEM((tk, 2 * tq), F32),
                        pltpu.VMEM((tk, 2 * tq), F32)],
        compiler_params=pltpu.CompilerParams(dimension_semantics=("arbitrary", "arbitrary", "arbitrary"),
                                             vmem_limit_bytes=VMEM_LIMIT),
        name="attn",
    )(lamp, q, kb, vb, subln_g)


def _sproj_kernel(x_ref, scale_ref, shift_ref, g_ref, w_ref, z_ref, h_scr):
    @pl.when(pl.program_id(0) == 0)
    def _():
        h_scr[...] = _modulated_norm(x_ref[...], g_ref[...], scale_ref[...], shift_ref[...]).astype(BF16)

    z_ref[...] = jnp.dot(h_scr[...], w_ref[...], preferred_element_type=F32)


def _sproj(x, scale, shift, norm_g, w_in):
    n = x.shape[0]
    full = pl.BlockSpec((n, D_MODEL), lambda j: (0, 0))
    return pl.pallas_call(
        _sproj_kernel,
        grid=(N_SEG,),
        in_specs=[full, full, full, pl.BlockSpec((1, D_MODEL), lambda j: (0, 0)),
                  pl.BlockSpec((D_MODEL, D_MODEL), lambda j: (0, j))],
        out_specs=pl.BlockSpec((n, D_MODEL), lambda j: (0, j)),
        out_shape=jax.ShapeDtypeStruct((n, D_IN), F32),
        scratch_shapes=[pltpu.VMEM((n, D_MODEL), BF16)],
        compiler_params=pltpu.CompilerParams(dimension_semantics=("arbitrary",)),
        name="sproj",
    )(x, scale, shift, norm_g, w_in)


def _sconv_kernel(za_ref, zb_ref, zc_ref, st_ref, cw_ref, cb_ref, lng_ref, lnb_ref,
                  ca_ref, cst_ref, u_scr, uext_scr, y_scr, *, n_seq, t_new):
    u_scr[...] = za_ref[...] * jax.nn.sigmoid(zb_ref[...])

    def one_seq(b, carry):
        r0 = pl.multiple_of(b * t_new, t_new)
        uext_scr[0:HALO, :] = st_ref[b]
        uext_scr[HALO:HALO + t_new, :] = u_scr[pl.ds(r0, t_new), :]
        blk = uext_scr[...]
        y_scr[pl.ds(r0, t_new), :] = _conv_rows(blk, cw_ref[...], t_new)
        cst_ref[b] = blk[HALO + t_new - (CONV_W - 1):HALO + t_new]
        return carry

    lax.fori_loop(0, n_seq, one_seq, 0)
    ca_ref[...] = _layernorm_gate(y_scr[...] + cb_ref[...], lng_ref[...], lnb_ref[...],
                                  zc_ref[...]).astype(BF16)


def _sconv(z, state, conv_w, conv_b, ln_g, ln_b, n_seq, t_new):
    n = n_seq * t_new
    zseg = lambda j: pl.BlockSpec((n, D_MODEL), lambda i: (0, j))
    const = lambda shape: pl.BlockSpec(shape, lambda i: (0,) * len(shape))
    return pl.pallas_call(
        functools.partial(_sconv_kernel, n_seq=n_seq, t_new=t_new),
        grid=(1,),
        in_specs=[zseg(4), zseg(5), zseg(6), const((n_seq, HALO, D_MODEL)),
                  const((CONV_W, D_MODEL)), const((1, D_MODEL)), const((1, D_MODEL)), const((1, D_MODEL))],
        out_specs=[const((n, D_MODEL)), const((n_seq, CONV_W - 1, D_MODEL))],
        out_shape=[jax.ShapeDtypeStruct((n, D_MODEL), BF16),
                   jax.ShapeDtypeStruct((n_seq, CONV_W - 1, D_MODEL), F32)],
        scratch_shapes=[pltpu.VMEM((n, D_MODEL), F32),
                        pltpu.VMEM((HALO + t_new, D_MODEL), F32),
                        pltpu.VMEM((n, D_MODEL), F32)],
        name="sconv",
    )(z, z, z, state, conv_w, conv_b, ln_g, ln_b)


def _decode_kernel(pt_ref, lamp_ref, q_ref, kn_ref, vn_ref, sg_ref, *rest, pp, t_new):
    del pt_ref
    k_refs, v_refs = rest[:pp], rest[pp:2 * pp]
    o_ref, qrow_scr, bias_scr, m_scr, l_scr, acc_scr = rest[2 * pp:]
    step = pl.program_id(1)
    n_rows = N_HEADS * 2 * t_new
    page_rows = PAGE * N_HEADS
    new_rows = t_new * N_HEADS
    log2_heads = N_HEADS.bit_length() - 1
    log2_pair = (2 * t_new).bit_length() - 1
    contract_last = (((1,), (1,)), ((), ()))

    def online_update(s, v, first):
        s_max = jnp.max(s, axis=1, keepdims=True)
        if first:
            m_new = s_max
            p = jnp.exp2(s - m_new)
            l_scr[...] = jnp.sum(p, axis=1, keepdims=True)
            acc_scr[...] = jnp.dot(p.astype(BF16), v, preferred_element_type=F32)
        else:
            m_prev = m_scr[...]
            m_new = jnp.maximum(m_prev, s_max)
            alpha = jnp.exp2(m_prev - m_new)
            p = jnp.exp2(s - m_new)
            l_scr[...] = alpha * l_scr[...] + jnp.sum(p, axis=1, keepdims=True)
            acc_scr[...] = alpha * acc_scr[...] + jnp.dot(p.astype(BF16), v, preferred_element_type=F32)
        m_scr[...] = m_new

    @pl.when(step == 0)
    def _():
        q = q_ref[...] * Q_SCALE
        lane = lax.broadcasted_iota(jnp.int32, (t_new, DV), 1)
        rows = []
        for h in range(N_HEADS):
            qh = q[:, h * DV:(h + 1) * DV]
            rows += [jnp.where(lane < DK, qh, 0.0), jnp.where(lane >= DK, qh, 0.0)]
        qrow_scr[...] = jnp.concatenate(rows, axis=0).astype(BF16)

        row = lax.broadcasted_iota(jnp.int32, (n_rows, page_rows), 0)
        col = lax.broadcasted_iota(jnp.int32, (n_rows, page_rows), 1)
        same_head = (col & (N_HEADS - 1)) == (row >> log2_pair)
        bias_scr[...] = jnp.where(same_head, 0.0, -jnp.inf)

        pad = jnp.zeros((PAGE - new_rows, DV), F32)
        kn = jnp.concatenate([kn_ref[...], pad], axis=0).astype(BF16)
        vn = jnp.concatenate([vn_ref[...], pad], axis=0).astype(BF16)
        s = lax.dot_general(qrow_scr[...], kn, contract_last, preferred_element_type=F32)
        row = lax.broadcasted_iota(jnp.int32, (n_rows, PAGE), 0)
        col = lax.broadcasted_iota(jnp.int32, (n_rows, PAGE), 1)
        visible = ((col & (N_HEADS - 1)) == (row >> log2_pair)) & ((col >> log2_heads) <= (row & (t_new - 1)))
        online_update(jnp.where(visible, s, -jnp.inf), vn, True)

    q_rows = qrow_scr[...]
    bias = bias_scr[...]
    s = jnp.concatenate(
        [lax.dot_general(q_rows, r[0].astype(BF16), contract_last, preferred_element_type=F32) + bias
         for r in k_refs], axis=1)
    v = jnp.concatenate([r[0] for r in v_refs], axis=0).astype(BF16)
    online_update(s, v, False)

    @pl.when(step == pl.num_programs(1) - 1)
    def _():
        lam = _lam(lamp_ref)
        o = acc_scr[...] / l_scr[...]
        for h in range(N_HEADS):
            r = h * 2 * t_new
            a = o[r:r + t_new] - lam * o[r + t_new:r + 2 * t_new]
            o_ref[:, h * DV:(h + 1) * DV] = _subln(a, sg_ref[...])


def _decode(page_table, lamp, q, k_new, v_new, cache_k, cache_v, subln_g, n_seq, t_new, pp):
    n_pages = page_table.shape[1]
    width = N_HEADS * DV
    page_rows = PAGE * N_HEADS
    new_rows = t_new * N_HEADS
    n_rows = N_HEADS * 2 * t_new
    assert t_new & (t_new - 1) == 0 and N_HEADS & (N_HEADS - 1) == 0
    assert n_pages % pp == 0 and new_rows <= PAGE
    page = lambda i: pl.BlockSpec((1, page_rows, DV), lambda b, s, pt: (pt[b, s * pp + i], 0, 0))
    new = lambda: pl.BlockSpec((new_rows, DV), lambda b, s, pt: (b, 0))
    grid_spec = pltpu.PrefetchScalarGridSpec(
        num_scalar_prefetch=1,
        grid=(n_seq, n_pages // pp),
        in_specs=[pl.BlockSpec((4, DK), lambda b, s, pt: (0, 0)),
                  pl.BlockSpec((t_new, width), lambda b, s, pt: (b, 0)), new(), new(),
                  pl.BlockSpec((1, DV), lambda b, s, pt: (0, 0))]
                 + [page(i) for i in range(pp)] + [page(i) for i in range(pp)],
        out_specs=pl.BlockSpec((t_new, width), lambda b, s, pt: (b, 0)),
        scratch_shapes=[pltpu.VMEM((n_rows, DV), BF16),
                        pltpu.VMEM((n_rows, page_rows), F32),
                        pltpu.VMEM((n_rows, 1), F32),
                        pltpu.VMEM((n_rows, 1), F32),
                        pltpu.VMEM((n_rows, DV), F32)])
    return pl.pallas_call(
        functools.partial(_decode_kernel, pp=pp, t_new=t_new),
        grid_spec=grid_spec,
        out_shape=jax.ShapeDtypeStruct((n_seq * t_new, width), F32),
        compiler_params=pltpu.CompilerParams(dimension_semantics=("arbitrary", "arbitrary"),
                                             vmem_limit_bytes=VMEM_LIMIT),
        name="decode",
    )(page_table, lamp, q, k_new, v_new, subln_g, *([cache_k] * pp), *([cache_v] * pp))


def _post_kernel(o_ref, ga_ref, ca_ref, gma_ref, gmc_ref, x_ref, gate_ref,
                 wa_ref, wc_ref, bc_ref, wo_ref, fg_ref, y_ref):
    a = (o_ref[0].astype(F32) * _silu(ga_ref[0].astype(F32))).astype(BF16)
    ya = jnp.dot(a, wa_ref[...], preferred_element_type=F32)
    yc = jnp.dot(ca_ref[0], wc_ref[...], preferred_element_type=F32) + bc_ref[...]
    m = (jax.nn.sigmoid(gma_ref[0].astype(F32)) * ya + jax.nn.sigmoid(gmc_ref[0].astype(F32)) * yc)
    d = jnp.dot(m.astype(BF16), wo_ref[...], preferred_element_type=F32)
    xo = x_ref[0] + gate_ref[0] * d
    ms = jnp.mean(xo * xo, axis=-1, keepdims=True)
    y_ref[0] = xo * lax.rsqrt(ms + EPS) * fg_ref[...]


def _post(o, ga, ga_seg, ca, gm, gm_seg, x, gate, w_attn_o, w_conv_o, b_conv_o, w_out, final_g, tm):
    B, S, _ = x.shape
    gate_rows = gate.shape[1]
    tile = lambda seg=0: pl.BlockSpec((1, tm, D_MODEL), lambda b, t: (b, t, seg))
    const = lambda shape: pl.BlockSpec(shape, lambda b, t: (0,) * len(shape))
    if gate_rows == 1:
        gate_spec = pl.BlockSpec((1, 1, D_MODEL), lambda b, t: (b, 0, 0))
    else:
        gate_spec = tile()
    return pl.pallas_call(
        _post_kernel,
        grid=(B, S // tm),
        in_specs=[tile(), tile(ga_seg), tile(), tile(gm_seg), tile(gm_seg + 1), tile(), gate_spec,
                  const((D_MODEL, D_MODEL)), const((D_MODEL, D_MODEL)), const((1, D_MODEL)),
                  const((D_MODEL, D_MODEL)), const((1, D_MODEL))],
        out_specs=tile(),
        out_shape=jax.ShapeDtypeStruct((B, S, D_MODEL), F32),
        compiler_params=pltpu.CompilerParams(dimension_semantics=("arbitrary", "arbitrary"),
                                             vmem_limit_bytes=VMEM_LIMIT),
        name="post",
    )(o, ga, ca, gm, gm, x, gate, w_attn_o, w_conv_o, b_conv_o, w_out, final_g)


def kernel(x_prompt, x_sample, cache_k, cache_v, state_conv, page_table, c_prompt, c_sample, w_ada, b_ada, norm_g, w_in, lambda_q1, lambda_k1, lambda_q2, lambda_k2, subln_g, w_attn_o, conv_w, conv_b, conv_ln_g, conv_ln_b, w_conv_o, b_conv_o, w_out, final_g):
    assert w_ada.shape[0] == 1, "single-layer stack"
    B, S, _ = x_prompt.shape
    n_seq, t_new, _ = x_sample.shape
    n_tok = n_seq * t_new
    width = N_HEADS * DV

    row = lambda p: p.reshape(1, -1)
    w_in_b = w_in[0].astype(BF16)
    w_attn_o_b, w_conv_o_b, w_out_b = (w[0].astype(BF16) for w in (w_attn_o, w_conv_o, w_out))
    norm_g0, conv_b0, ln_g0, ln_b0 = row(norm_g[0]), row(conv_b[0]), row(conv_ln_g[0]), row(conv_ln_b[0])
    b_conv_o0, final_g0, subln_g0 = row(b_conv_o[0]), row(final_g), row(subln_g[0])
    lamp = jnp.stack([lambda_q1[0], lambda_k1[0], lambda_q2[0], lambda_k2[0]])

    n_c = B + n_seq
    pad = -n_c % 8
    c_all = jnp.concatenate([c_prompt, c_sample, jnp.zeros((pad, D_MODEL), F32)], axis=0)
    mod = _ada(c_all, w_ada[0], row(b_ada[0]))
    shift, scale, gate = jnp.split(mod[:n_c], 3, axis=-1)

    p_mod = lambda m: m[:B].reshape(B, 1, D_MODEL)
    q, k, kb, v, vb, ga, ca, gm, conv_prompt = _front(
        x_prompt, p_mod(scale), p_mod(shift), norm_g0, w_in_b, conv_w[0], conv_b0, ln_g0, ln_b0, tm=256)
    o = _attention(lamp, q, kb, vb, subln_g0, tq=512, tk=256, unroll=4)
    y_prompt = _post(o, ga, 0, ca, gm, 0, x_prompt, p_mod(gate),
                     w_attn_o_b, w_conv_o_b, b_conv_o0, w_out_b, final_g0, tm=512)

    s_mod = lambda m: jnp.repeat(m[B:], t_new, axis=0)
    xs = x_sample.reshape(n_tok, D_MODEL)
    z = _sproj(xs, s_mod(scale), s_mod(shift), norm_g0, w_in_b)
    state = jnp.pad(state_conv[0], ((0, 0), (HALO - (CONV_W - 1), 0), (0, 0)))
    ca_s, conv_sample = _sconv(z, state, conv_w[0], conv_b0, ln_g0, ln_b0, n_seq, t_new)
    heads = lambda a, lead: a.reshape(1, *lead, N_HEADS, DV)
    k_sample = heads(z[:, width:2 * width], (n_seq, t_new))
    v_sample = heads(z[:, 2 * width:3 * width], (n_seq, t_new))
    n_phys = cache_k.shape[1]
    o_s = _decode(page_table, lamp, z, k_sample.reshape(-1, DV), v_sample.reshape(-1, DV),
                  cache_k[0].reshape(n_phys, PAGE * N_HEADS, DV), cache_v[0].reshape(n_phys, PAGE * N_HEADS, DV),
                  subln_g0, n_seq, t_new, pp=8)
    z3 = z.reshape(1, n_tok, D_IN)
    y_sample = _post(o_s.reshape(1, n_tok, width), z3, 3, ca_s.reshape(1, n_tok, D_MODEL), z3, 7,
                     xs.reshape(1, n_tok, D_MODEL), s_mod(gate).reshape(1, n_tok, D_MODEL),
                     w_attn_o_b, w_conv_o_b, b_conv_o0, w_out_b, final_g0, tm=n_tok)

    return (y_prompt, y_sample.reshape(n_seq, t_new, D_MODEL),
            heads(k, (B, S)), heads(v, (B, S)), conv_prompt[None],
            k_sample, v_sample, conv_sample[None])
```

```python
import functools
import math

import jax
import jax.numpy as jnp
from jax import lax
from jax.experimental import pallas as pl
from jax.experimental.pallas import tpu as pltpu

F32 = jnp.float32
BF16 = jnp.bfloat16

D_MODEL = 1024
N_HEADS = 8
DK = 64
DV = 2 * DK
CONV_W = 31
PAGE = 128
EPS = 1e-6
N_SEG = 9
D_IN = N_SEG * D_MODEL
LAM_INIT = 0.8 - 0.6 * math.exp(-0.3 * 0)
LOG2E = 1.4426950408889634
Q_SCALE = DK ** -0.5 * LOG2E

ONES_ROWS = 16
HALO = 32
VMEM_LIMIT = 56 * 1024 * 1024


def _silu(x):
    return x * jax.nn.sigmoid(x)


def _modulated_norm(x, g, scale, shift):
    ms = jnp.mean(x * x, axis=-1, keepdims=True)
    return (x * lax.rsqrt(ms + EPS) * g) * (1.0 + scale) + shift


def _layernorm_gate(y, ln_g, ln_b, gc):
    mu = jnp.mean(y, axis=-1, keepdims=True)
    yc = y - mu
    var = jnp.mean(yc * yc, axis=-1, keepdims=True)
    ln = yc * lax.rsqrt(var + EPS) * ln_g + ln_b
    return _silu(ln) * _silu(gc)


def _conv_rows(blk, cw, rc):
    base = HALO - (CONV_W - 1)
    acc = None
    n = rc + HALO
    for r in range(8):
        sh = blk if r == 0 else pltpu.roll(blk, n - r, axis=0)
        for a in range(HALO // 8 + 1):
            w = 8 * a + r - base
            if 0 <= w < CONV_W:
                term = sh[8 * a:8 * a + rc] * cw[w:w + 1]
                acc = term if acc is None else acc + term
    return acc


def _lam(lamp_ref):
    lp = lamp_ref[...]
    a = jnp.sum(lp[0:1] * lp[1:2], axis=-1, keepdims=True)
    b = jnp.sum(lp[2:3] * lp[3:4], axis=-1, keepdims=True)
    return jnp.exp(a) - jnp.exp(b) + LAM_INIT


def _subln(a, sg):
    ms = jnp.mean(a * a, axis=-1, keepdims=True)
    return a * lax.rsqrt(ms + EPS) * sg * (1.0 - LAM_INIT)


def _ada_kernel(c_ref, w_ref, b_ref, o_ref):
    a = _silu(c_ref[...]).astype(BF16)
    o_ref[...] = jnp.dot(a, w_ref[...].astype(BF16), preferred_element_type=F32) + b_ref[...]


def _ada(c_all, w_ada, b_ada):
    rows = c_all.shape[0]
    tn = 768
    return pl.pallas_call(
        _ada_kernel,
        grid=(3 * D_MODEL // tn,),
        in_specs=[pl.BlockSpec((rows, D_MODEL), lambda j: (0, 0)),
                  pl.BlockSpec((D_MODEL, tn), lambda j: (0, j)),
                  pl.BlockSpec((1, tn), lambda j: (0, j))],
        out_specs=pl.BlockSpec((rows, tn), lambda j: (0, j)),
        out_shape=jax.ShapeDtypeStruct((rows, 3 * D_MODEL), F32),
        name="ada",
    )(c_all, w_ada, b_ada)


def _front_kernel(x_ref, scale_ref, shift_ref, g_ref, w_ref, cw_ref, cb_ref, lng_ref, lnb_ref,
                  q_ref, k_ref, kb_ref, v_ref, vb_ref, ga_ref, ca_ref, gm_ref, cst_ref,
                  h_scr, uext_scr, y_scr, *, tm):
    t = pl.program_id(1)
    h_scr[...] = _modulated_norm(x_ref[0], g_ref[...], scale_ref[0], shift_ref[0]).astype(BF16)

    def seg(j, n=1):
        return jnp.dot(h_scr[...], w_ref[:, j * D_MODEL:(j + n) * D_MODEL], preferred_element_type=F32)

    @pl.when(t == 0)
    def _():
        uext_scr[0:HALO, :] = jnp.zeros((HALO, D_MODEL), F32)

    uext_scr[HALO:HALO + tm, :] = seg(4) * jax.nn.sigmoid(seg(5))

    rc = 64

    def conv_rows(ci, carry):
        r0 = pl.multiple_of(ci * rc, rc)
        for lc in range(D_MODEL // 128):
            ls = slice(lc * 128, (lc + 1) * 128)
            y_scr[pl.ds(r0, rc), ls] = _conv_rows(uext_scr[pl.ds(r0, rc + HALO), ls], cw_ref[:, ls], rc)
        return carry

    lax.fori_loop(0, tm // rc, conv_rows, 0)

    q_ref[0] = (seg(0) * Q_SCALE).astype(BF16)
    zk = seg(1)
    k_ref[0] = zk
    kb_ref[0] = zk.astype(BF16)
    zv = seg(2)
    v_ref[0] = zv
    vb_ref[0] = zv.astype(BF16)
    ga_ref[0] = seg(3).astype(BF16)
    gm_ref[0] = seg(7, 2).astype(BF16)
    ca_ref[0] = _layernorm_gate(y_scr[...] + cb_ref[...], lng_ref[...], lnb_ref[...], seg(6)).astype(BF16)

    last = uext_scr[tm:tm + HALO, :]
    uext_scr[0:HALO, :] = last

    @pl.when(t == pl.num_programs(1) - 1)
    def _():
        cst_ref[0] = last[HALO - (CONV_W - 1):HALO]


def _front(x, scale, shift, norm_g, w_in, conv_w, conv_b, ln_g, ln_b, tm):
    B, S, _ = x.shape
    tile = lambda width: pl.BlockSpec((1, tm, width), lambda b, t: (b, t, 0))
    row = lambda: pl.BlockSpec((1, 1, D_MODEL), lambda b, t: (b, 0, 0))
    const = lambda shape: pl.BlockSpec(shape, lambda b, t: (0,) * len(shape))
    bsd = lambda width, dt: jax.ShapeDtypeStruct((B, S, width), dt)
    return pl.pallas_call(
        functools.partial(_front_kernel, tm=tm),
        grid=(B, S // tm),
        in_specs=[tile(D_MODEL), row(), row(), const((1, D_MODEL)),
                  pl.BlockSpec((D_MODEL, D_IN), lambda b, t: (0, 0), pipeline_mode=pl.Buffered(1)),
                  const((CONV_W, D_MODEL)), const((1, D_MODEL)), const((1, D_MODEL)), const((1, D_MODEL))],
        out_specs=[tile(D_MODEL), tile(D_MODEL), tile(D_MODEL), tile(D_MODEL), tile(D_MODEL),
                   tile(D_MODEL), tile(D_MODEL), tile(2 * D_MODEL),
                   pl.BlockSpec((1, CONV_W - 1, D_MODEL), lambda b, t: (b, 0, 0))],
        out_shape=[bsd(D_MODEL, BF16), bsd(D_MODEL, F32), bsd(D_MODEL, BF16), bsd(D_MODEL, F32),
                   bsd(D_MODEL, BF16), bsd(D_MODEL, BF16), bsd(D_MODEL, BF16), bsd(2 * D_MODEL, BF16),
                   jax.ShapeDtypeStruct((B, CONV_W - 1, D_MODEL), F32)],
        scratch_shapes=[pltpu.VMEM((tm, D_MODEL), BF16),
                        pltpu.VMEM((HALO + tm, D_MODEL), F32),
                        pltpu.VMEM((tm, D_MODEL), F32)],
        compiler_params=pltpu.CompilerParams(dimension_semantics=("arbitrary", "arbitrary"),
                                             vmem_limit_bytes=VMEM_LIMIT),
        name="front",
    )(x, scale, shift, norm_g, w_in, conv_w, conv_b, ln_g, ln_b)


def _attn_kernel(lamp_ref, q_ref, k_ref, v_ref, sg_ref, o_ref, qbd_scr, vt_scr, acc_scr, m_scr, *s_scrs, tq, tk, unroll):
    i = pl.program_id(2)
    n_kv = v_ref.shape[1] // tk
    n_diag = tq // tk
    n_full = i * n_diag
    UNROLL = unroll

    @pl.when(i == 0)
    def _():
        extra = lax.broadcasted_iota(jnp.int32, (ONES_ROWS, tk), 0)
        ones_rows = jnp.where(extra == 0, 1.0, 0.0)

        def transpose_v(c, carry):
            c0 = pl.multiple_of(c * tk, tk)
            vt = v_ref[0, pl.ds(c0, tk), :].astype(F32).T
            vt_scr[c] = jnp.concatenate([vt, ones_rows], axis=0).astype(BF16)
            return carry
        lax.fori_loop(0, n_kv, transpose_v, 0)

    qt = q_ref[0].astype(F32).T
    depth = lax.broadcasted_iota(jnp.int32, (DV, tq), 0)
    qbd_scr[...] = jnp.concatenate([jnp.where(depth < DK, qt, 0.0), jnp.where(depth >= DK, qt, 0.0)],
                                   axis=1).astype(BF16)
    acc_scr[...] = jnp.zeros((DV + ONES_ROWS, 2 * tq), F32)

    def scores(j):
        k0 = pl.multiple_of(j * tk, tk)
        return jnp.dot(k_ref[0, pl.ds(k0, tk), :], qbd_scr[...], preferred_element_type=F32)

    def update(s, j):
        m_prev = m_scr[...]
        m_new = jnp.maximum(m_prev, jnp.max(s, axis=0, keepdims=True))
        alpha = jnp.exp2(m_prev - m_new)
        p = jnp.exp2(s - m_new).astype(BF16)
        acc_scr[...] = alpha * acc_scr[...] + jnp.dot(vt_scr[j], p, preferred_element_type=F32)
        m_scr[...] = m_new

    def past_tiles(j0, count):
        for t in range(count):
            s_scrs[1 - t % 2][...] = scores(j0 + t + 1)
            update(s_scrs[t % 2][...], j0 + t)

    m_scr[...] = jnp.full((1, 2 * tq), -jnp.inf, F32)
    s_scrs[0][...] = scores(0)

    def body(jj, carry):
        past_tiles(UNROLL * jj, UNROLL)
        return carry

    n_main = n_full // UNROLL
    lax.fori_loop(0, n_main, body, 0)

    if UNROLL > 2:
        def rest_body(jj, carry):
            past_tiles(n_main * UNROLL + 2 * jj, 2)
            return carry

        lax.fori_loop(0, (n_full - n_main * UNROLL) // 2, rest_body, 0)

    key = lax.broadcasted_iota(jnp.int32, (tk, 2 * tq), 0)
    qry = lax.broadcasted_iota(jnp.int32, (tk, 2 * tq), 1)
    qry = jnp.where(qry >= tq, qry - tq, qry)
    for d in range(n_diag):
        if d + 1 < n_diag:
            s_scrs[(d + 1) % 2][...] = scores(n_full + d + 1)
        update(jnp.where(key + d * tk <= qry, s_scrs[d % 2][...], -jnp.inf), n_full + d)

    ot = acc_scr[0:DV, :] / acc_scr[DV:DV + 1, :]
    a = ot[:, 0:tq] - _lam(lamp_ref) * ot[:, tq:2 * tq]
    ms = jnp.mean(a * a, axis=0, keepdims=True)
    o_ref[0] = ((a * lax.rsqrt(ms + EPS)).T * sg_ref[...] * (1.0 - LAM_INIT)).astype(BF16)


def _attention(lamp, q, kb, vb, subln_g, tq, tk, unroll):
    B, S, _ = q.shape
    assert tq == 2 * tk and S % tq == 0 and unroll % 2 == 0
    return pl.pallas_call(
        functools.partial(_attn_kernel, tq=tq, tk=tk, unroll=unroll),
        grid=(B, N_HEADS, S // tq),
        in_specs=[pl.BlockSpec((4, DK), lambda b, h, i: (0, 0)),
                  pl.BlockSpec((1, tq, DV), lambda b, h, i: (b, i, h)),
                  pl.BlockSpec((1, S, DV), lambda b, h, i: (b, 0, h)),
                  pl.BlockSpec((1, S, DV), lambda b, h, i: (b, 0, h)),
                  pl.BlockSpec((1, DV), lambda b, h, i: (0, 0))],
        out_specs=pl.BlockSpec((1, tq, DV), lambda b, h, i: (b, i, h)),
        out_shape=jax.ShapeDtypeStruct((B, S, N_HEADS * DV), BF16),
        scratch_shapes=[pltpu.VMEM((DV, 2 * tq), BF16),
                        pltpu.VMEM((S // tk, DV + ONES_ROWS, tk), BF16),
                        pltpu.VMEM((DV + ONES_ROWS, 2 * tq), F32),
                        pltpu.VMEM((1, 2 * tq), F32),
                        pltpu.VMEM((tk, 2 * tq), F32),
                        pltpu.VMEM((tk, 2 * tq), F32)],
        compiler_params=pltpu.CompilerParams(dimension_semantics=("arbitrary", "arbitrary", "arbitrary"),
                                             vmem_limit_bytes=VMEM_LIMIT),
        name="attn",
    )(lamp, q, kb, vb, subln_g)


def _sproj_kernel(x_ref, scale_ref, shift_ref, g_ref, w_ref, z_ref, h_scr):
    @pl.when(pl.program_id(0) == 0)
    def _():
        h_scr[...] = _modulated_norm(x_ref[...], g_ref[...], scale_ref[...], shift_ref[...]).astype(BF16)

    z_ref[...] = jnp.dot(h_scr[...], w_ref[...], preferred_element_type=F32)


def _sproj(x, scale, shift, norm_g, w_in):
    n = x.shape[0]
    full = pl.BlockSpec((n, D_MODEL), lambda j: (0, 0))
    return pl.pallas_call(
        _sproj_kernel,
        grid=(N_SEG,),
        in_specs=[full, full, full, pl.BlockSpec((1, D_MODEL), lambda j: (0, 0)),
                  pl.BlockSpec((D_MODEL, D_MODEL), lambda j: (0, j))],
        out_specs=pl.BlockSpec((n, D_MODEL), lambda j: (0, j)),
        out_shape=jax.ShapeDtypeStruct((n, D_IN), F32),
        scratch_shapes=[pltpu.VMEM((n, D_MODEL), BF16)],
        compiler_params=pltpu.CompilerParams(dimension_semantics=("arbitrary",)),
        name="sproj",
    )(x, scale, shift, norm_g, w_in)


def _sconv_kernel(za_ref, zb_ref, zc_ref, st_ref, cw_ref, cb_ref, lng_ref, lnb_ref,
                  ca_ref, cst_ref, u_scr, uext_scr, y_scr, *, n_seq, t_new):
    u_scr[...] = za_ref[...] * jax.nn.sigmoid(zb_ref[...])

    def one_seq(b, carry):
        r0 = pl.multiple_of(b * t_new, t_new)
        uext_scr[0:HALO, :] = st_ref[b]
        uext_scr[HALO:HALO + t_new, :] = u_scr[pl.ds(r0, t_new), :]
        blk = uext_scr[...]
        y_scr[pl.ds(r0, t_new), :] = _conv_rows(blk, cw_ref[...], t_new)
        cst_ref[b] = blk[HALO + t_new - (CONV_W - 1):HALO + t_new]
        return carry

    lax.fori_loop(0, n_seq, one_seq, 0)
    ca_ref[...] = _layernorm_gate(y_scr[...] + cb_ref[...], lng_ref[...], lnb_ref[...],
                                  zc_ref[...]).astype(BF16)


def _sconv(z, state, conv_w, conv_b, ln_g, ln_b, n_seq, t_new):
    n = n_seq * t_new
    zseg = lambda j: pl.BlockSpec((n, D_MODEL), lambda i: (0, j))
    const = lambda shape: pl.BlockSpec(shape, lambda i: (0,) * len(shape))
    return pl.pallas_call(
        functools.partial(_sconv_kernel, n_seq=n_seq, t_new=t_new),
        grid=(1,),
        in_specs=[zseg(4), zseg(5), zseg(6), const((n_seq, HALO, D_MODEL)),
                  const((CONV_W, D_MODEL)), const((1, D_MODEL)), const((1, D_MODEL)), const((1, D_MODEL))],
        out_specs=[const((n, D_MODEL)), const((n_seq, CONV_W - 1, D_MODEL))],
        out_shape=[jax.ShapeDtypeStruct((n, D_MODEL), BF16),
                   jax.ShapeDtypeStruct((n_seq, CONV_W - 1, D_MODEL), F32)],
        scratch_shapes=[pltpu.VMEM((n, D_MODEL), F32),
                        pltpu.VMEM((HALO + t_new, D_MODEL), F32),
                        pltpu.VMEM((n, D_MODEL), F32)],
        name="sconv",
    )(z, z, z, state, conv_w, conv_b, ln_g, ln_b)


def _decode_kernel(pt_ref, lamp_ref, q_ref, kn_ref, vn_ref, sg_ref, *rest, pp, t_new, n_groups):
    del pt_ref
    k_refs, v_refs = rest[:pp], rest[pp:2 * pp]
    o_ref, qrow_scr, bias_scr, m_scr, l_scr, acc_scr, s_scr = rest[2 * pp:]
    step = pl.program_id(1)
    n_rows = N_HEADS * 2 * t_new
    page_rows = PAGE * N_HEADS
    new_rows = t_new * N_HEADS
    log2_heads = N_HEADS.bit_length() - 1
    log2_pair = (2 * t_new).bit_length() - 1
    contract_last = (((1,), (1,)), ((), ()))

    def online_update(s, v, first):
        s_max = jnp.max(s, axis=1, keepdims=True)
        if first:
            m_new = s_max
            p = jnp.exp2(s - m_new)
            l_scr[...] = jnp.sum(p, axis=1, keepdims=True)
            acc_scr[...] = jnp.dot(p.astype(BF16), v, preferred_element_type=F32)
        else:
            m_prev = m_scr[...]
            m_new = jnp.maximum(m_prev, s_max)
            alpha = jnp.exp2(m_prev - m_new)
            p = jnp.exp2(s - m_new)
            l_scr[...] = alpha * l_scr[...] + jnp.sum(p, axis=1, keepdims=True)
            acc_scr[...] = alpha * acc_scr[...] + jnp.dot(p.astype(BF16), v, preferred_element_type=F32)
        m_scr[...] = m_new

    @pl.when(step == 0)
    def _():
        q = q_ref[...] * Q_SCALE
        lane = lax.broadcasted_iota(jnp.int32, (t_new, DV), 1)
        rows = []
        for h in range(N_HEADS):
            qh = q[:, h * DV:(h + 1) * DV]
            rows += [jnp.where(lane < DK, qh, 0.0), jnp.where(lane >= DK, qh, 0.0)]
        qrow_scr[...] = jnp.concatenate(rows, axis=0).astype(BF16)

        row = lax.broadcasted_iota(jnp.int32, (n_rows, page_rows), 0)
        col = lax.broadcasted_iota(jnp.int32, (n_rows, page_rows), 1)
        same_head = (col & (N_HEADS - 1)) == (row >> log2_pair)
        bias_scr[...] = jnp.where(same_head, 0.0, -jnp.inf)

        pad = jnp.zeros((PAGE - new_rows, DV), F32)
        kn = jnp.concatenate([kn_ref[...], pad], axis=0).astype(BF16)
        vn = jnp.concatenate([vn_ref[...], pad], axis=0).astype(BF16)
        s = lax.dot_general(qrow_scr[...], kn, contract_last, preferred_element_type=F32)
        row = lax.broadcasted_iota(jnp.int32, (n_rows, PAGE), 0)
        col = lax.broadcasted_iota(jnp.int32, (n_rows, PAGE), 1)
        visible = ((col & (N_HEADS - 1)) == (row >> log2_pair)) & ((col >> log2_heads) <= (row & (t_new - 1)))
        online_update(jnp.where(visible, s, -jnp.inf), vn, True)

    q_rows = qrow_scr[...]
    bias = bias_scr[...]

    def page_scores(i):
        s_scr[:, i * page_rows:(i + 1) * page_rows] = lax.dot_general(
            q_rows, k_refs[i][0].astype(BF16), contract_last, preferred_element_type=F32) + bias

    per_group = pp // n_groups
    for i in range(per_group):
        page_scores(i)
    for g in range(n_groups):
        if g + 1 < n_groups:
            for i in range((g + 1) * per_group, (g + 2) * per_group):
                page_scores(i)
        pages = slice(g * per_group, (g + 1) * per_group)
        v = jnp.concatenate([r[0] for r in v_refs[pages]], axis=0).astype(BF16)
        online_update(s_scr[:, g * per_group * page_rows:(g + 1) * per_group * page_rows], v, False)

    @pl.when(step == pl.num_programs(1) - 1)
    def _():
        lam = _lam(lamp_ref)
        o = acc_scr[...] / l_scr[...]
        for h in range(N_HEADS):
            r = h * 2 * t_new
            a = o[r:r + t_new] - lam * o[r + t_new:r + 2 * t_new]
            o_ref[:, h * DV:(h + 1) * DV] = _subln(a, sg_ref[...])


def _decode(page_table, lamp, q, k_new, v_new, cache_k, cache_v, subln_g, n_seq, t_new, pp):
    n_pages = page_table.shape[1]
    width = N_HEADS * DV
    page_rows = PAGE * N_HEADS
    new_rows = t_new * N_HEADS
    n_rows = N_HEADS * 2 * t_new
    assert t_new & (t_new - 1) == 0 and N_HEADS & (N_HEADS - 1) == 0
    assert n_pages % pp == 0 and new_rows <= PAGE
    page = lambda i: pl.BlockSpec((1, page_rows, DV), lambda b, s, pt: (pt[b, s * pp + i], 0, 0))
    new = lambda: pl.BlockSpec((new_rows, DV), lambda b, s, pt: (b, 0))
    grid_spec = pltpu.PrefetchScalarGridSpec(
        num_scalar_prefetch=1,
        grid=(n_seq, n_pages // pp),
        in_specs=[pl.BlockSpec((4, DK), lambda b, s, pt: (0, 0)),
                  pl.BlockSpec((t_new, width), lambda b, s, pt: (b, 0)), new(), new(),
                  pl.BlockSpec((1, DV), lambda b, s, pt: (0, 0))]
                 + [page(i) for i in range(pp)] + [page(i) for i in range(pp)],
        out_specs=pl.BlockSpec((t_new, width), lambda b, s, pt: (b, 0)),
        scratch_shapes=[pltpu.VMEM((n_rows, DV), BF16),
                        pltpu.VMEM((n_rows, page_rows), F32),
                        pltpu.VMEM((n_rows, 1), F32),
                        pltpu.VMEM((n_rows, 1), F32),
                        pltpu.VMEM((n_rows, DV), F32),
                        pltpu.VMEM((n_rows, pp * page_rows), F32)])
    return pl.pallas_call(
        functools.partial(_decode_kernel, pp=pp, t_new=t_new, n_groups=2),
        grid_spec=grid_spec,
        out_shape=jax.ShapeDtypeStruct((n_seq * t_new, width), F32),
        compiler_params=pltpu.CompilerParams(dimension_semantics=("arbitrary", "arbitrary"),
                                             vmem_limit_bytes=VMEM_LIMIT),
        name="decode",
    )(page_table, lamp, q, k_new, v_new, subln_g, *([cache_k] * pp), *([cache_v] * pp))


def _post_kernel(o_ref, ga_ref, ca_ref, gma_ref, gmc_ref, x_ref, gate_ref,
                 wa_ref, wc_ref, bc_ref, wo_ref, fg_ref, y_ref):
    a = (o_ref[0].astype(F32) * _silu(ga_ref[0].astype(F32))).astype(BF16)
    ya = jnp.dot(a, wa_ref[...], preferred_element_type=F32)
    yc = jnp.dot(ca_ref[0], wc_ref[...], preferred_element_type=F32) + bc_ref[...]
    m = (jax.nn.sigmoid(gma_ref[0].astype(F32)) * ya + jax.nn.sigmoid(gmc_ref[0].astype(F32)) * yc)
    d = jnp.dot(m.astype(BF16), wo_ref[...], preferred_element_type=F32)
    xo = x_ref[0] + gate_ref[0] * d
    ms = jnp.mean(xo * xo, axis=-1, keepdims=True)
    y_ref[0] = xo * lax.rsqrt(ms + EPS) * fg_ref[...]


def _post(o, ga, ga_seg, ca, gm, gm_seg, x, gate, w_attn_o, w_conv_o, b_conv_o, w_out, final_g, tm):
    B, S, _ = x.shape
    gate_rows = gate.shape[1]
    tile = lambda seg=0: pl.BlockSpec((1, tm, D_MODEL), lambda b, t: (b, t, seg))
    const = lambda shape: pl.BlockSpec(shape, lambda b, t: (0,) * len(shape))
    if gate_rows == 1:
        gate_spec = pl.BlockSpec((1, 1, D_MODEL), lambda b, t: (b, 0, 0))
    else:
        gate_spec = tile()
    return pl.pallas_call(
        _post_kernel,
        grid=(B, S // tm),
        in_specs=[tile(), tile(ga_seg), tile(), tile(gm_seg), tile(gm_seg + 1), tile(), gate_spec,
                  const((D_MODEL, D_MODEL)), const((D_MODEL, D_MODEL)), const((1, D_MODEL)),
                  const((D_MODEL, D_MODEL)), const((1, D_MODEL))],
        out_specs=tile(),
        out_shape=jax.ShapeDtypeStruct((B, S, D_MODEL), F32),
        compiler_params=pltpu.CompilerParams(dimension_semantics=("arbitrary", "arbitrary"),
                                             vmem_limit_bytes=VMEM_LIMIT),
        name="post",
    )(o, ga, ca, gm, gm, x, gate, w_attn_o, w_conv_o, b_conv_o, w_out, final_g)


def kernel(x_prompt, x_sample, cache_k, cache_v, state_conv, page_table, c_prompt, c_sample, w_ada, b_ada, norm_g, w_in, lambda_q1, lambda_k1, lambda_q2, lambda_k2, subln_g, w_attn_o, conv_w, conv_b, conv_ln_g, conv_ln_b, w_conv_o, b_conv_o, w_out, final_g):
    assert w_ada.shape[0] == 1, "single-layer stack"
    B, S, _ = x_prompt.shape
    n_seq, t_new, _ = x_sample.shape
    n_tok = n_seq * t_new
    width = N_HEADS * DV

    row = lambda p: p.reshape(1, -1)
    w_in_b = w_in[0].astype(BF16)
    w_attn_o_b, w_conv_o_b, w_out_b = (w[0].astype(BF16) for w in (w_attn_o, w_conv_o, w_out))
    norm_g0, conv_b0, ln_g0, ln_b0 = row(norm_g[0]), row(conv_b[0]), row(conv_ln_g[0]), row(conv_ln_b[0])
    b_conv_o0, final_g0, subln_g0 = row(b_conv_o[0]), row(final_g), row(subln_g[0])
    lamp = jnp.stack([lambda_q1[0], lambda_k1[0], lambda_q2[0], lambda_k2[0]])

    n_c = B + n_seq
    pad = -n_c % 8
    c_all = jnp.concatenate([c_prompt, c_sample, jnp.zeros((pad, D_MODEL), F32)], axis=0)
    mod = _ada(c_all, w_ada[0], row(b_ada[0]))
    shift, scale, gate = jnp.split(mod[:n_c], 3, axis=-1)

    p_mod = lambda m: m[:B].reshape(B, 1, D_MODEL)
    q, k, kb, v, vb, ga, ca, gm, conv_prompt = _front(
        x_prompt, p_mod(scale), p_mod(shift), norm_g0, w_in_b, conv_w[0], conv_b0, ln_g0, ln_b0, tm=256)
    o = _attention(lamp, q, kb, vb, subln_g0, tq=512, tk=256, unroll=8)
    y_prompt = _post(o, ga, 0, ca, gm, 0, x_prompt, p_mod(gate),
                     w_attn_o_b, w_conv_o_b, b_conv_o0, w_out_b, final_g0, tm=512)

    s_mod = lambda m: jnp.repeat(m[B:], t_new, axis=0)
    xs = x_sample.reshape(n_tok, D_MODEL)
    z = _sproj(xs, s_mod(scale), s_mod(shift), norm_g0, w_in_b)
    state = jnp.pad(state_conv[0], ((0, 0), (HALO - (CONV_W - 1), 0), (0, 0)))
    ca_s, conv_sample = _sconv(z, state, conv_w[0], conv_b0, ln_g0, ln_b0, n_seq, t_new)
    heads = lambda a, lead: a.reshape(1, *lead, N_HEADS, DV)
    k_sample = heads(z[:, width:2 * width], (n_seq, t_new))
    v_sample = heads(z[:, 2 * width:3 * width], (n_seq, t_new))
    n_phys = cache_k.shape[1]
    o_s = _decode(page_table, lamp, z, k_sample.reshape(-1, DV), v_sample.reshape(-1, DV),
                  cache_k[0].reshape(n_phys, PAGE * N_HEADS, DV), cache_v[0].reshape(n_phys, PAGE * N_HEADS, DV),
                  subln_g0, n_seq, t_new, pp=8)
    z3 = z.reshape(1, n_tok, D_IN)
    y_sample = _post(o_s.reshape(1, n_tok, width), z3, 3, ca_s.reshape(1, n_tok, D_MODEL), z3, 7,
                     xs.reshape(1, n_tok, D_MODEL), s_mod(gate).reshape(1, n_tok, D_MODEL),
                     w_attn_o_b, w_conv_o_b, b_conv_o0, w_out_b, final_g0, tm=n_tok)

    return (y_prompt, y_sample.reshape(n_seq, t_new, D_MODEL),
            heads(k, (B, S)), heads(v, (B, S)), conv_prompt[None],
            k_sample, v_sample, conv_sample[None])
```

```python
import functools
import math

import jax
import jax.numpy as jnp
from jax import lax
from jax.experimental import pallas as pl
from jax.experimental.pallas import tpu as pltpu

F32 = jnp.float32
BF16 = jnp.bfloat16

D_MODEL = 1024
N_HEADS = 8
DK = 64
DV = 2 * DK
CONV_W = 31
PAGE = 128
EPS = 1e-6
N_SEG = 9
D_IN = N_SEG * D_MODEL
LAM_INIT = 0.8 - 0.6 * math.exp(-0.3 * 0)
LOG2E = 1.4426950408889634
Q_SCALE = DK ** -0.5 * LOG2E

ONES_ROWS = 16
HALO = 32
VMEM_LIMIT = 56 * 1024 * 1024


def _silu(x):
    return x * jax.nn.sigmoid(x)


def _modulated_norm(x, g, scale, shift):
    ms = jnp.mean(x * x, axis=-1, keepdims=True)
    return (x * lax.rsqrt(ms + EPS) * g) * (1.0 + scale) + shift


def _layernorm_gate(y, ln_g, ln_b, gc):
    mu = jnp.mean(y, axis=-1, keepdims=True)
    yc = y - mu
    var = jnp.mean(yc * yc, axis=-1, keepdims=True)
    ln = yc * lax.rsqrt(var + EPS) * ln_g + ln_b
    return _silu(ln) * _silu(gc)


def _conv_rows(blk, cw, rc):
    base = HALO - (CONV_W - 1)
    acc = None
    n = rc + HALO
    for r in range(8):
        sh = blk if r == 0 else pltpu.roll(blk, n - r, axis=0)
        for a in range(HALO // 8 + 1):
            w = 8 * a + r - base
            if 0 <= w < CONV_W:
                term = sh[8 * a:8 * a + rc] * cw[w:w + 1]
                acc = term if acc is None else acc + term
    return acc


def _lam(lamp_ref):
    lp = lamp_ref[...]
    a = jnp.sum(lp[0:1] * lp[1:2], axis=-1, keepdims=True)
    b = jnp.sum(lp[2:3] * lp[3:4], axis=-1, keepdims=True)
    return jnp.exp(a) - jnp.exp(b) + LAM_INIT


def _subln(a, sg):
    ms = jnp.mean(a * a, axis=-1, keepdims=True)
    return a * lax.rsqrt(ms + EPS) * sg * (1.0 - LAM_INIT)


def _ada_kernel(c_ref, w_ref, b_ref, o_ref):
    a = _silu(c_ref[...]).astype(BF16)
    o_ref[...] = jnp.dot(a, w_ref[...].astype(BF16), preferred_element_type=F32) + b_ref[...]


def _ada(c_all, w_ada, b_ada):
    rows = c_all.shape[0]
    tn = 768
    return pl.pallas_call(
        _ada_kernel,
        grid=(3 * D_MODEL // tn,),
        in_specs=[pl.BlockSpec((rows, D_MODEL), lambda j: (0, 0)),
                  pl.BlockSpec((D_MODEL, tn), lambda j: (0, j)),
                  pl.BlockSpec((1, tn), lambda j: (0, j))],
        out_specs=pl.BlockSpec((rows, tn), lambda j: (0, j)),
        out_shape=jax.ShapeDtypeStruct((rows, 3 * D_MODEL), F32),
        name="ada",
    )(c_all, w_ada, b_ada)


def _front_kernel(x_ref, scale_ref, shift_ref, g_ref, w_ref, cw_ref, cb_ref, lng_ref, lnb_ref,
                  qt_ref, k_ref, kb_ref, v_ref, vt_ref, ga_ref, ca_ref, gm_ref, cst_ref,
                  h_scr, uext_scr, y_scr, *, tm):
    t = pl.program_id(1)
    h_scr[...] = _modulated_norm(x_ref[0], g_ref[...], scale_ref[0], shift_ref[0]).astype(BF16)

    def seg(j, n=1):
        return jnp.dot(h_scr[...], w_ref[:, j * D_MODEL:(j + n) * D_MODEL], preferred_element_type=F32)

    @pl.when(t == 0)
    def _():
        uext_scr[0:HALO, :] = jnp.zeros((HALO, D_MODEL), F32)

    uext_scr[HALO:HALO + tm, :] = seg(4) * jax.nn.sigmoid(seg(5))

    rc = 64

    def conv_rows(ci, carry):
        r0 = pl.multiple_of(ci * rc, rc)
        for lc in range(D_MODEL // 128):
            ls = slice(lc * 128, (lc + 1) * 128)
            y_scr[pl.ds(r0, rc), ls] = _conv_rows(uext_scr[pl.ds(r0, rc + HALO), ls], cw_ref[:, ls], rc)
        return carry

    lax.fori_loop(0, tm // rc, conv_rows, 0)

    zq = seg(0) * Q_SCALE
    depth = lax.broadcasted_iota(jnp.int32, (DV, tm), 0)
    for h in range(N_HEADS):
        qt = zq[:, h * DV:(h + 1) * DV].T
        qt_ref[0, h, 0] = jnp.where(depth < DK, qt, 0.0).astype(BF16)
        qt_ref[0, h, 1] = jnp.where(depth >= DK, qt, 0.0).astype(BF16)
    zk = seg(1)
    k_ref[0] = zk
    kb_ref[0] = zk.astype(BF16)
    zv = seg(2)
    v_ref[0] = zv
    extra = lax.broadcasted_iota(jnp.int32, (ONES_ROWS, tm), 0)
    ones_rows = jnp.where(extra == 0, 1.0, 0.0)
    for h in range(N_HEADS):
        vt_ref[0, h, 0] = jnp.concatenate([zv[:, h * DV:(h + 1) * DV].T, ones_rows], axis=0).astype(BF16)
    ga_ref[0] = seg(3).astype(BF16)
    gm_ref[0] = seg(7, 2).astype(BF16)
    ca_ref[0] = _layernorm_gate(y_scr[...] + cb_ref[...], lng_ref[...], lnb_ref[...], seg(6)).astype(BF16)

    last = uext_scr[tm:tm + HALO, :]
    uext_scr[0:HALO, :] = last

    @pl.when(t == pl.num_programs(1) - 1)
    def _():
        cst_ref[0] = last[HALO - (CONV_W - 1):HALO]


def _front(x, scale, shift, norm_g, w_in, conv_w, conv_b, ln_g, ln_b, tm):
    B, S, _ = x.shape
    tile = lambda width: pl.BlockSpec((1, tm, width), lambda b, t: (b, t, 0))
    row = lambda: pl.BlockSpec((1, 1, D_MODEL), lambda b, t: (b, 0, 0))
    const = lambda shape: pl.BlockSpec(shape, lambda b, t: (0,) * len(shape))
    bsd = lambda width, dt: jax.ShapeDtypeStruct((B, S, width), dt)
    return pl.pallas_call(
        functools.partial(_front_kernel, tm=tm),
        grid=(B, S // tm),
        in_specs=[tile(D_MODEL), row(), row(), const((1, D_MODEL)),
                  pl.BlockSpec((D_MODEL, D_IN), lambda b, t: (0, 0), pipeline_mode=pl.Buffered(1)),
                  const((CONV_W, D_MODEL)), const((1, D_MODEL)), const((1, D_MODEL)), const((1, D_MODEL))],
        out_specs=[pl.BlockSpec((1, N_HEADS, 2, DV, tm), lambda b, t: (b, 0, 0, 0, t)),
                   tile(D_MODEL), tile(D_MODEL), tile(D_MODEL),
                   pl.BlockSpec((1, N_HEADS, 1, DV + ONES_ROWS, tm), lambda b, t: (b, 0, t, 0, 0)),
                   tile(D_MODEL), tile(D_MODEL), tile(2 * D_MODEL),
                   pl.BlockSpec((1, CONV_W - 1, D_MODEL), lambda b, t: (b, 0, 0))],
        out_shape=[jax.ShapeDtypeStruct((B, N_HEADS, 2, DV, S), BF16),
                   bsd(D_MODEL, F32), bsd(D_MODEL, BF16), bsd(D_MODEL, F32),
                   jax.ShapeDtypeStruct((B, N_HEADS, S // tm, DV + ONES_ROWS, tm), BF16),
                   bsd(D_MODEL, BF16), bsd(D_MODEL, BF16), bsd(2 * D_MODEL, BF16),
                   jax.ShapeDtypeStruct((B, CONV_W - 1, D_MODEL), F32)],
        scratch_shapes=[pltpu.VMEM((tm, D_MODEL), BF16),
                        pltpu.VMEM((HALO + tm, D_MODEL), F32),
                        pltpu.VMEM((tm, D_MODEL), F32)],
        compiler_params=pltpu.CompilerParams(dimension_semantics=("arbitrary", "arbitrary"),
                                             vmem_limit_bytes=VMEM_LIMIT),
        name="front",
    )(x, scale, shift, norm_g, w_in, conv_w, conv_b, ln_g, ln_b)


def _attn_kernel(lamp_ref, qt_ref, k_ref, vt_ref, sg_ref, o_ref, qbd_scr, acc_scr, m_scr, *s_scrs, tq, tk, unroll):
    i = pl.program_id(2)
    n_diag = tq // tk
    n_full = i * n_diag
    UNROLL = unroll

    qbd_scr[...] = jnp.concatenate([qt_ref[0, 0, 0], qt_ref[0, 0, 1]], axis=1)
    acc_scr[...] = jnp.zeros((DV + ONES_ROWS, 2 * tq), F32)

    def scores(j):
        k0 = pl.multiple_of(j * tk, tk)
        return jnp.dot(k_ref[0, pl.ds(k0, tk), :], qbd_scr[...], preferred_element_type=F32)

    def update(s, j):
        m_prev = m_scr[...]
        m_new = jnp.maximum(m_prev, jnp.max(s, axis=0, keepdims=True))
        alpha = jnp.exp2(m_prev - m_new)
        p = jnp.exp2(s - m_new).astype(BF16)
        acc_scr[...] = alpha * acc_scr[...] + jnp.dot(vt_ref[0, 0, j], p, preferred_element_type=F32)
        m_scr[...] = m_new

    def past_tiles(j0, count):
        for t in range(count):
            s_scrs[1 - t % 2][...] = scores(j0 + t + 1)
            update(s_scrs[t % 2][...], j0 + t)

    m_scr[...] = jnp.full((1, 2 * tq), -jnp.inf, F32)
    s_scrs[0][...] = scores(0)

    def body(jj, carry):
        past_tiles(UNROLL * jj, UNROLL)
        return carry

    n_main = n_full // UNROLL
    lax.fori_loop(0, n_main, body, 0)

    if UNROLL > 2:
        def rest_body(jj, carry):
            past_tiles(n_main * UNROLL + 2 * jj, 2)
            return carry

        lax.fori_loop(0, (n_full - n_main * UNROLL) // 2, rest_body, 0)

    key = lax.broadcasted_iota(jnp.int32, (tk, 2 * tq), 0)
    qry = lax.broadcasted_iota(jnp.int32, (tk, 2 * tq), 1)
    qry = jnp.where(qry >= tq, qry - tq, qry)
    for d in range(n_diag):
        if d + 1 < n_diag:
            s_scrs[(d + 1) % 2][...] = scores(n_full + d + 1)
        update(jnp.where(key + d * tk <= qry, s_scrs[d % 2][...], -jnp.inf), n_full + d)

    ot = acc_scr[0:DV, :] / acc_scr[DV:DV + 1, :]
    a = ot[:, 0:tq] - _lam(lamp_ref) * ot[:, tq:2 * tq]
    ms = jnp.mean(a * a, axis=0, keepdims=True)
    o_ref[0] = ((a * lax.rsqrt(ms + EPS)).T * sg_ref[...] * (1.0 - LAM_INIT)).astype(BF16)


def _attention(lamp, qt, kb, vt, subln_g, tq, unroll):
    B, S, _ = kb.shape
    tk = vt.shape[-1]
    assert tq == 2 * tk and S % tq == 0 and unroll % 2 == 0
    return pl.pallas_call(
        functools.partial(_attn_kernel, tq=tq, tk=tk, unroll=unroll),
        grid=(B, N_HEADS, S // tq),
        in_specs=[pl.BlockSpec((4, DK), lambda b, h, i: (0, 0)),
                  pl.BlockSpec((1, 1, 2, DV, tq), lambda b, h, i: (b, h, 0, 0, i)),
                  pl.BlockSpec((1, S, DV), lambda b, h, i: (b, 0, h)),
                  pl.BlockSpec((1, 1, S // tk, DV + ONES_ROWS, tk), lambda b, h, i: (b, h, 0, 0, 0)),
                  pl.BlockSpec((1, DV), lambda b, h, i: (0, 0))],
        out_specs=pl.BlockSpec((1, tq, DV), lambda b, h, i: (b, i, h)),
        out_shape=jax.ShapeDtypeStruct((B, S, N_HEADS * DV), BF16),
        scratch_shapes=[pltpu.VMEM((DV, 2 * tq), BF16),
                        pltpu.VMEM((DV + ONES_ROWS, 2 * tq), F32),
                        pltpu.VMEM((1, 2 * tq), F32),
                        pltpu.VMEM((tk, 2 * tq), F32),
                        pltpu.VMEM((tk, 2 * tq), F32)],
        compiler_params=pltpu.CompilerParams(dimension_semantics=("arbitrary", "arbitrary", "arbitrary"),
                                             vmem_limit_bytes=VMEM_LIMIT),
        name="attn",
    )(lamp, qt, kb, vt, subln_g)


def _sproj_kernel(x_ref, scale_ref, shift_ref, g_ref, w_ref, z_ref, wb_ref, h_scr):
    @pl.when(pl.program_id(0) == 0)
    def _():
        h_scr[...] = _modulated_norm(x_ref[...], g_ref[...], scale_ref[...], shift_ref[...]).astype(BF16)

    wb = w_ref[...].astype(BF16)
    wb_ref[...] = wb
    z_ref[...] = jnp.dot(h_scr[...], wb, preferred_element_type=F32)


def _sproj(x, scale, shift, norm_g, w_in):
    n = x.shape[0]
    full = pl.BlockSpec((n, D_MODEL), lambda j: (0, 0))
    wseg = pl.BlockSpec((D_MODEL, D_MODEL), lambda j: (0, j))
    return pl.pallas_call(
        _sproj_kernel,
        grid=(N_SEG,),
        in_specs=[full, full, full, pl.BlockSpec((1, D_MODEL), lambda j: (0, 0)), wseg],
        out_specs=[pl.BlockSpec((n, D_MODEL), lambda j: (0, j)), wseg],
        out_shape=[jax.ShapeDtypeStruct((n, D_IN), F32), jax.ShapeDtypeStruct((D_MODEL, D_IN), BF16)],
        scratch_shapes=[pltpu.VMEM((n, D_MODEL), BF16)],
        compiler_params=pltpu.CompilerParams(dimension_semantics=("arbitrary",)),
        name="sproj",
    )(x, scale, shift, norm_g, w_in)


def _sconv_kernel(za_ref, zb_ref, zc_ref, st_ref, cw_ref, cb_ref, lng_ref, lnb_ref,
                  ca_ref, cst_ref, u_scr, uext_scr, y_scr, *, n_seq, t_new):
    u_scr[...] = za_ref[...] * jax.nn.sigmoid(zb_ref[...])

    def one_seq(b, carry):
        r0 = pl.multiple_of(b * t_new, t_new)
        uext_scr[0:HALO, :] = st_ref[b]
        uext_scr[HALO:HALO + t_new, :] = u_scr[pl.ds(r0, t_new), :]
        blk = uext_scr[...]
        y_scr[pl.ds(r0, t_new), :] = _conv_rows(blk, cw_ref[...], t_new)
        cst_ref[b] = blk[HALO + t_new - (CONV_W - 1):HALO + t_new]
        return carry

    lax.fori_loop(0, n_seq, one_seq, 0)
    ca_ref[...] = _layernorm_gate(y_scr[...] + cb_ref[...], lng_ref[...], lnb_ref[...],
                                  zc_ref[...]).astype(BF16)


def _sconv(z, state, conv_w, conv_b, ln_g, ln_b, n_seq, t_new):
    n = n_seq * t_new
    zseg = lambda j: pl.BlockSpec((n, D_MODEL), lambda i: (0, j))
    const = lambda shape: pl.BlockSpec(shape, lambda i: (0,) * len(shape))
    return pl.pallas_call(
        functools.partial(_sconv_kernel, n_seq=n_seq, t_new=t_new),
        grid=(1,),
        in_specs=[zseg(4), zseg(5), zseg(6), const((n_seq, HALO, D_MODEL)),
                  const((CONV_W, D_MODEL)), const((1, D_MODEL)), const((1, D_MODEL)), const((1, D_MODEL))],
        out_specs=[const((n, D_MODEL)), const((n_seq, CONV_W - 1, D_MODEL))],
        out_shape=[jax.ShapeDtypeStruct((n, D_MODEL), BF16),
                   jax.ShapeDtypeStruct((n_seq, CONV_W - 1, D_MODEL), F32)],
        scratch_shapes=[pltpu.VMEM((n, D_MODEL), F32),
                        pltpu.VMEM((HALO + t_new, D_MODEL), F32),
                        pltpu.VMEM((n, D_MODEL), F32)],
        name="sconv",
    )(z, z, z, state, conv_w, conv_b, ln_g, ln_b)


def _decode_kernel(pt_ref, lamp_ref, q_ref, kn_ref, vn_ref, sg_ref, *rest, pp, t_new, n_groups):
    del pt_ref
    k_refs, v_refs = rest[:pp], rest[pp:2 * pp]
    o_ref, qrow_scr, bias_scr, m_scr, l_scr, acc_scr, s_scr = rest[2 * pp:]
    step = pl.program_id(1)
    n_rows = N_HEADS * 2 * t_new
    page_rows = PAGE * N_HEADS
    new_rows = t_new * N_HEADS
    log2_heads = N_HEADS.bit_length() - 1
    log2_pair = (2 * t_new).bit_length() - 1
    contract_last = (((1,), (1,)), ((), ()))

    def online_update(s, v, first):
        s_max = jnp.max(s, axis=1, keepdims=True)
        if first:
            m_new = s_max
            p = jnp.exp2(s - m_new)
            l_scr[...] = jnp.sum(p, axis=1, keepdims=True)
            acc_scr[...] = jnp.dot(p.astype(BF16), v, preferred_element_type=F32)
        else:
            m_prev = m_scr[...]
            m_new = jnp.maximum(m_prev, s_max)
            alpha = jnp.exp2(m_prev - m_new)
            p = jnp.exp2(s - m_new)
            l_scr[...] = alpha * l_scr[...] + jnp.sum(p, axis=1, keepdims=True)
            acc_scr[...] = alpha * acc_scr[...] + jnp.dot(p.astype(BF16), v, preferred_element_type=F32)
        m_scr[...] = m_new

    @pl.when(step == 0)
    def _():
        q = q_ref[...] * Q_SCALE
        lane = lax.broadcasted_iota(jnp.int32, (t_new, DV), 1)
        rows = []
        for h in range(N_HEADS):
            qh = q[:, h * DV:(h + 1) * DV]
            rows += [jnp.where(lane < DK, qh, 0.0), jnp.where(lane >= DK, qh, 0.0)]
        qrow_scr[...] = jnp.concatenate(rows, axis=0).astype(BF16)

        row = lax.broadcasted_iota(jnp.int32, (n_rows, page_rows), 0)
        col = lax.broadcasted_iota(jnp.int32, (n_rows, page_rows), 1)
        same_head = (col & (N_HEADS - 1)) == (row >> log2_pair)
        bias_scr[...] = jnp.where(same_head, 0.0, -jnp.inf)

        pad = jnp.zeros((PAGE - new_rows, DV), F32)
        kn = jnp.concatenate([kn_ref[...], pad], axis=0).astype(BF16)
        vn = jnp.concatenate([vn_ref[...], pad], axis=0).astype(BF16)
        s = lax.dot_general(qrow_scr[...], kn, contract_last, preferred_element_type=F32)
        row = lax.broadcasted_iota(jnp.int32, (n_rows, PAGE), 0)
        col = lax.broadcasted_iota(jnp.int32, (n_rows, PAGE), 1)
        visible = ((col & (N_HEADS - 1)) == (row >> log2_pair)) & ((col >> log2_heads) <= (row & (t_new - 1)))
        online_update(jnp.where(visible, s, -jnp.inf), vn, True)

    q_rows = qrow_scr[...]
    bias = bias_scr[...]

    def page_scores(i):
        s_scr[:, i * page_rows:(i + 1) * page_rows] = lax.dot_general(
            q_rows, k_refs[i][0].astype(BF16), contract_last, preferred_element_type=F32) + bias

    per_group = pp // n_groups
    for i in range(per_group):
        page_scores(i)
    for g in range(n_groups):
        if g + 1 < n_groups:
            for i in range((g + 1) * per_group, (g + 2) * per_group):
                page_scores(i)
        pages = slice(g * per_group, (g + 1) * per_group)
        v = jnp.concatenate([r[0] for r in v_refs[pages]], axis=0).astype(BF16)
        online_update(s_scr[:, g * per_group * page_rows:(g + 1) * per_group * page_rows], v, False)

    @pl.when(step == pl.num_programs(1) - 1)
    def _():
        lam = _lam(lamp_ref)
        o = acc_scr[...] / l_scr[...]
        for h in range(N_HEADS):
            r = h * 2 * t_new
            a = o[r:r + t_new] - lam * o[r + t_new:r + 2 * t_new]
            o_ref[:, h * DV:(h + 1) * DV] = _subln(a, sg_ref[...])


def _decode(page_table, lamp, q, k_new, v_new, cache_k, cache_v, subln_g, n_seq, t_new, pp):
    n_pages = page_table.shape[1]
    width = N_HEADS * DV
    page_rows = PAGE * N_HEADS
    new_rows = t_new * N_HEADS
    n_rows = N_HEADS * 2 * t_new
    assert t_new & (t_new - 1) == 0 and N_HEADS & (N_HEADS - 1) == 0
    assert n_pages % pp == 0 and new_rows <= PAGE
    page = lambda i: pl.BlockSpec((1, page_rows, DV), lambda b, s, pt: (pt[b, s * pp + i], 0, 0))
    new = lambda: pl.BlockSpec((new_rows, DV), lambda b, s, pt: (b, 0))
    grid_spec = pltpu.PrefetchScalarGridSpec(
        num_scalar_prefetch=1,
        grid=(n_seq, n_pages // pp),
        in_specs=[pl.BlockSpec((4, DK), lambda b, s, pt: (0, 0)),
                  pl.BlockSpec((t_new, width), lambda b, s, pt: (b, 0)), new(), new(),
                  pl.BlockSpec((1, DV), lambda b, s, pt: (0, 0))]
                 + [page(i) for i in range(pp)] + [page(i) for i in range(pp)],
        out_specs=pl.BlockSpec((t_new, width), lambda b, s, pt: (b, 0)),
        scratch_shapes=[pltpu.VMEM((n_rows, DV), BF16),
                        pltpu.VMEM((n_rows, page_rows), F32),
                        pltpu.VMEM((n_rows, 1), F32),
                        pltpu.VMEM((n_rows, 1), F32),
                        pltpu.VMEM((n_rows, DV), F32),
                        pltpu.VMEM((n_rows, pp * page_rows), F32)])
    return pl.pallas_call(
        functools.partial(_decode_kernel, pp=pp, t_new=t_new, n_groups=pp // 4),
        grid_spec=grid_spec,
        out_shape=jax.ShapeDtypeStruct((n_seq * t_new, width), F32),
        compiler_params=pltpu.CompilerParams(dimension_semantics=("arbitrary", "arbitrary"),
                                             vmem_limit_bytes=VMEM_LIMIT),
        name="decode",
    )(page_table, lamp, q, k_new, v_new, subln_g, *([cache_k] * pp), *([cache_v] * pp))


def _post_kernel(o_ref, ga_ref, ca_ref, gma_ref, gmc_ref, x_ref, gate_ref,
                 wa_ref, wc_ref, bc_ref, wo_ref, fg_ref, y_ref):
    a = (o_ref[0].astype(F32) * _silu(ga_ref[0].astype(F32))).astype(BF16)
    ya = jnp.dot(a, wa_ref[...], preferred_element_type=F32)
    yc = jnp.dot(ca_ref[0], wc_ref[...], preferred_element_type=F32) + bc_ref[...]
    m = (jax.nn.sigmoid(gma_ref[0].astype(F32)) * ya + jax.nn.sigmoid(gmc_ref[0].astype(F32)) * yc)
    d = jnp.dot(m.astype(BF16), wo_ref[...], preferred_element_type=F32)
    xo = x_ref[0] + gate_ref[0] * d
    ms = jnp.mean(xo * xo, axis=-1, keepdims=True)
    y_ref[0] = xo * lax.rsqrt(ms + EPS) * fg_ref[...]


def _post(o, ga, ga_seg, ca, gm, gm_seg, x, gate, w_attn_o, w_conv_o, b_conv_o, w_out, final_g, tm):
    B, S, _ = x.shape
    gate_rows = gate.shape[1]
    tile = lambda seg=0: pl.BlockSpec((1, tm, D_MODEL), lambda b, t: (b, t, seg))
    const = lambda shape: pl.BlockSpec(shape, lambda b, t: (0,) * len(shape))
    if gate_rows == 1:
        gate_spec = pl.BlockSpec((1, 1, D_MODEL), lambda b, t: (b, 0, 0))
    else:
        gate_spec = tile()
    return pl.pallas_call(
        _post_kernel,
        grid=(B, S // tm),
        in_specs=[tile(), tile(ga_seg), tile(), tile(gm_seg), tile(gm_seg + 1), tile(), gate_spec,
                  const((D_MODEL, D_MODEL)), const((D_MODEL, D_MODEL)), const((1, D_MODEL)),
                  const((D_MODEL, D_MODEL)), const((1, D_MODEL))],
        out_specs=tile(),
        out_shape=jax.ShapeDtypeStruct((B, S, D_MODEL), F32),
        compiler_params=pltpu.CompilerParams(dimension_semantics=("arbitrary", "arbitrary"),
                                             vmem_limit_bytes=VMEM_LIMIT),
        name="post",
    )(o, ga, ca, gm, gm, x, gate, w_attn_o, w_conv_o, b_conv_o, w_out, final_g)


def kernel(x_prompt, x_sample, cache_k, cache_v, state_conv, page_table, c_prompt, c_sample, w_ada, b_ada, norm_g, w_in, lambda_q1, lambda_k1, lambda_q2, lambda_k2, subln_g, w_attn_o, conv_w, conv_b, conv_ln_g, conv_ln_b, w_conv_o, b_conv_o, w_out, final_g):
    assert w_ada.shape[0] == 1, "single-layer stack"
    B, S, _ = x_prompt.shape
    n_seq, t_new, _ = x_sample.shape
    n_tok = n_seq * t_new
    width = N_HEADS * DV

    row = lambda p: p.reshape(1, -1)
    w_attn_o_b, w_conv_o_b, w_out_b = (w[0].astype(BF16) for w in (w_attn_o, w_conv_o, w_out))
    norm_g0, conv_b0, ln_g0, ln_b0 = row(norm_g[0]), row(conv_b[0]), row(conv_ln_g[0]), row(conv_ln_b[0])
    b_conv_o0, final_g0, subln_g0 = row(b_conv_o[0]), row(final_g), row(subln_g[0])
    lamp = jnp.stack([lambda_q1[0], lambda_k1[0], lambda_q2[0], lambda_k2[0]])

    n_c = B + n_seq
    pad = -n_c % 8
    c_all = jnp.concatenate([c_prompt, c_sample, jnp.zeros((pad, D_MODEL), F32)], axis=0)
    mod = _ada(c_all, w_ada[0], row(b_ada[0]))
    shift, scale, gate = jnp.split(mod[:n_c], 3, axis=-1)

    s_mod = lambda m: jnp.repeat(m[B:], t_new, axis=0)
    xs = x_sample.reshape(n_tok, D_MODEL)
    z, w_in_b = _sproj(xs, s_mod(scale), s_mod(shift), norm_g0, w_in[0])

    p_mod = lambda m: m[:B].reshape(B, 1, D_MODEL)
    qt, k, kb, v, vt, ga, ca, gm, conv_prompt = _front(
        x_prompt, p_mod(scale), p_mod(shift), norm_g0, w_in_b, conv_w[0], conv_b0, ln_g0, ln_b0, tm=256)
    o = _attention(lamp, qt, kb, vt, subln_g0, tq=512, unroll=8)
    y_prompt = _post(o, ga, 0, ca, gm, 0, x_prompt, p_mod(gate),
                     w_attn_o_b, w_conv_o_b, b_conv_o0, w_out_b, final_g0, tm=512)

    state = jnp.pad(state_conv[0], ((0, 0), (HALO - (CONV_W - 1), 0), (0, 0)))
    ca_s, conv_sample = _sconv(z, state, conv_w[0], conv_b0, ln_g0, ln_b0, n_seq, t_new)
    heads = lambda a, lead: a.reshape(1, *lead, N_HEADS, DV)
    k_sample = heads(z[:, width:2 * width], (n_seq, t_new))
    v_sample = heads(z[:, 2 * width:3 * width], (n_seq, t_new))
    n_phys = cache_k.shape[1]
    o_s = _decode(page_table, lamp, z, k_sample.reshape(-1, DV), v_sample.reshape(-1, DV),
                  cache_k[0].reshape(n_phys, PAGE * N_HEADS, DV), cache_v[0].reshape(n_phys, PAGE * N_HEADS, DV),
                  subln_g0, n_seq, t_new, pp=16)
    z3 = z.reshape(1, n_tok, D_IN)
    y_sample = _post(o_s.reshape(1, n_tok, width), z3, 3, ca_s.reshape(1, n_tok, D_MODEL), z3, 7,
                     xs.reshape(1, n_tok, D_MODEL), s_mod(gate).reshape(1, n_tok, D_MODEL),
                     w_attn_o_b, w_conv_o_b, b_conv_o0, w_out_b, final_g0, tm=n_tok)

    return (y_prompt, y_sample.reshape(n_seq, t_new, D_MODEL),
            heads(k, (B, S)), heads(v, (B, S)), conv_prompt[None],
            k_sample, v_sample, conv_sample[None])
```

```python
import functools
import math

import jax
import jax.numpy as jnp
from jax import lax
from jax.experimental import pallas as pl
from jax.experimental.pallas import tpu as pltpu

F32 = jnp.float32
BF16 = jnp.bfloat16

D_MODEL = 1024
N_HEADS = 8
DK = 64
DV = 2 * DK
CONV_W = 31
PAGE = 128
EPS = 1e-6
N_SEG = 9
D_IN = N_SEG * D_MODEL
LAM_INIT = 0.8 - 0.6 * math.exp(-0.3 * 0)
LOG2E = 1.4426950408889634
Q_SCALE = DK ** -0.5 * LOG2E

SUBLANES, LANES = 8, 128
ONES_ROWS = 2 * SUBLANES
HALO = 4 * SUBLANES
VMEM_LIMIT = 56 * 1024 * 1024

FRONT_TM = 256
CONV_ROWS = 64
ATTN_TQ = 2 * FRONT_TM
ATTN_HEADS = 2
ATTN_UNROLL = 4
POST_TM = 512
DECODE_PAGES = 16
DECODE_GROUP = 4
ADA_TN = 6 * LANES


def _silu(x):
    return x * jax.nn.sigmoid(x)


def _modulated_norm(x, g, scale, shift):
    ms = jnp.mean(x * x, axis=-1, keepdims=True)
    return (x * lax.rsqrt(ms + EPS) * g) * (1.0 + scale) + shift


def _layernorm_gate(y, ln_g, ln_b, gc):
    mu = jnp.mean(y, axis=-1, keepdims=True)
    yc = y - mu
    var = jnp.mean(yc * yc, axis=-1, keepdims=True)
    ln = yc * lax.rsqrt(var + EPS) * ln_g + ln_b
    return _silu(ln) * _silu(gc)


def _conv_rows(blk, cw, rc):
    base = HALO - (CONV_W - 1)
    acc = None
    n = rc + HALO
    for r in range(SUBLANES):
        sh = blk if r == 0 else pltpu.roll(blk, n - r, axis=0)
        for a in range(HALO // SUBLANES + 1):
            w = SUBLANES * a + r - base
            if 0 <= w < CONV_W:
                term = sh[SUBLANES * a:SUBLANES * a + rc] * cw[w:w + 1]
                acc = term if acc is None else acc + term
    return acc


def _lam(lamp_ref):
    lp = lamp_ref[...]
    a = jnp.sum(lp[0:1] * lp[1:2], axis=-1, keepdims=True)
    b = jnp.sum(lp[2:3] * lp[3:4], axis=-1, keepdims=True)
    return jnp.exp(a) - jnp.exp(b) + LAM_INIT


def _subln(a, sg):
    ms = jnp.mean(a * a, axis=-1, keepdims=True)
    return a * lax.rsqrt(ms + EPS) * sg * (1.0 - LAM_INIT)


def _ada_kernel(c_ref, w_ref, b_ref, o_ref):
    a = _silu(c_ref[...]).astype(BF16)
    o_ref[...] = jnp.dot(a, w_ref[...].astype(BF16), preferred_element_type=F32) + b_ref[...]


def _ada(c_all, w_ada, b_ada):
    rows = c_all.shape[0]
    tn = ADA_TN
    return pl.pallas_call(
        _ada_kernel,
        grid=(3 * D_MODEL // tn,),
        in_specs=[pl.BlockSpec((rows, D_MODEL), lambda j: (0, 0)),
                  pl.BlockSpec((D_MODEL, tn), lambda j: (0, j)),
                  pl.BlockSpec((1, tn), lambda j: (0, j))],
        out_specs=pl.BlockSpec((rows, tn), lambda j: (0, j)),
        out_shape=jax.ShapeDtypeStruct((rows, 3 * D_MODEL), F32),
        name="ada",
    )(c_all, w_ada, b_ada)


def _front_kernel(x_ref, scale_ref, shift_ref, g_ref, w_ref, cw_ref, cb_ref, lng_ref, lnb_ref,
                  qt_ref, k_ref, kb_ref, v_ref, vt_ref, ga_ref, ca_ref, gm_ref, cst_ref,
                  h_scr, uext_scr, y_scr, *, tm):
    t = pl.program_id(1)
    h_scr[...] = _modulated_norm(x_ref[0], g_ref[...], scale_ref[0], shift_ref[0]).astype(BF16)

    def seg(j, n=1):
        return jnp.dot(h_scr[...], w_ref[:, j * D_MODEL:(j + n) * D_MODEL], preferred_element_type=F32)

    @pl.when(t == 0)
    def _():
        uext_scr[0:HALO, :] = jnp.zeros((HALO, D_MODEL), F32)

    uext_scr[HALO:HALO + tm, :] = seg(4) * jax.nn.sigmoid(seg(5))

    rc = CONV_ROWS

    def conv_rows(ci, carry):
        r0 = pl.multiple_of(ci * rc, rc)
        for lc in range(D_MODEL // LANES):
            ls = slice(lc * LANES, (lc + 1) * LANES)
            y_scr[pl.ds(r0, rc), ls] = _conv_rows(uext_scr[pl.ds(r0, rc + HALO), ls], cw_ref[:, ls], rc)
        return carry

    lax.fori_loop(0, tm // rc, conv_rows, 0)

    zq = seg(0) * Q_SCALE
    depth = lax.broadcasted_iota(jnp.int32, (DV, tm), 0)
    for h in range(N_HEADS):
        qt = zq[:, h * DV:(h + 1) * DV].T
        qt_ref[0, h, 0] = jnp.where(depth < DK, qt, 0.0).astype(BF16)
        qt_ref[0, h, 1] = jnp.where(depth >= DK, qt, 0.0).astype(BF16)
    zk = seg(1)
    k_ref[0] = zk
    kb_ref[0] = zk.astype(BF16)
    zv = seg(2)
    v_ref[0] = zv
    extra = lax.broadcasted_iota(jnp.int32, (ONES_ROWS, tm), 0)
    ones_rows = jnp.where(extra == 0, 1.0, 0.0)
    for h in range(N_HEADS):
        vt_ref[0, h, 0] = jnp.concatenate([zv[:, h * DV:(h + 1) * DV].T, ones_rows], axis=0).astype(BF16)
    ga_ref[0] = seg(3).astype(BF16)
    gm_ref[0] = seg(7, 2).astype(BF16)
    ca_ref[0] = _layernorm_gate(y_scr[...] + cb_ref[...], lng_ref[...], lnb_ref[...], seg(6)).astype(BF16)

    last = uext_scr[tm:tm + HALO, :]
    uext_scr[0:HALO, :] = last

    @pl.when(t == pl.num_programs(1) - 1)
    def _():
        cst_ref[0] = last[HALO - (CONV_W - 1):HALO]


def _front(x, scale, shift, norm_g, w_in, conv_w, conv_b, ln_g, ln_b, tm):
    B, S, _ = x.shape
    tile = lambda width: pl.BlockSpec((1, tm, width), lambda b, t: (b, t, 0))
    row = lambda: pl.BlockSpec((1, 1, D_MODEL), lambda b, t: (b, 0, 0))
    const = lambda shape: pl.BlockSpec(shape, lambda b, t: (0,) * len(shape))
    bsd = lambda width, dt: jax.ShapeDtypeStruct((B, S, width), dt)
    return pl.pallas_call(
        functools.partial(_front_kernel, tm=tm),
        grid=(B, S // tm),
        in_specs=[tile(D_MODEL), row(), row(), const((1, D_MODEL)),
                  pl.BlockSpec((D_MODEL, D_IN), lambda b, t: (0, 0), pipeline_mode=pl.Buffered(1)),
                  const((CONV_W, D_MODEL)), const((1, D_MODEL)), const((1, D_MODEL)), const((1, D_MODEL))],
        out_specs=[pl.BlockSpec((1, N_HEADS, 2, DV, tm), lambda b, t: (b, 0, 0, 0, t)),
                   tile(D_MODEL), tile(D_MODEL), tile(D_MODEL),
                   pl.BlockSpec((1, N_HEADS, 1, DV + ONES_ROWS, tm), lambda b, t: (b, 0, t, 0, 0)),
                   tile(D_MODEL), tile(D_MODEL), tile(2 * D_MODEL),
                   pl.BlockSpec((1, CONV_W - 1, D_MODEL), lambda b, t: (b, 0, 0))],
        out_shape=[jax.ShapeDtypeStruct((B, N_HEADS, 2, DV, S), BF16),
                   bsd(D_MODEL, F32), bsd(D_MODEL, BF16), bsd(D_MODEL, F32),
                   jax.ShapeDtypeStruct((B, N_HEADS, S // tm, DV + ONES_ROWS, tm), BF16),
                   bsd(D_MODEL, BF16), bsd(D_MODEL, BF16), bsd(2 * D_MODEL, BF16),
                   jax.ShapeDtypeStruct((B, CONV_W - 1, D_MODEL), F32)],
        scratch_shapes=[pltpu.VMEM((tm, D_MODEL), BF16),
                        pltpu.VMEM((HALO + tm, D_MODEL), F32),
                        pltpu.VMEM((tm, D_MODEL), F32)],
        compiler_params=pltpu.CompilerParams(dimension_semantics=("arbitrary", "arbitrary"),
                                             vmem_limit_bytes=VMEM_LIMIT),
        name="front",
    )(x, scale, shift, norm_g, w_in, conv_w, conv_b, ln_g, ln_b)


def _attn_kernel(lamp_ref, qt_ref, k_ref, vt_ref, sg_ref, o_ref, qbd_scr, acc_scr, m_scr, *s_scrs,
                 tq, tk, unroll, hpb):
    i = pl.program_id(2)
    n_diag = tq // tk
    n_full = i * n_diag
    heads = range(hpb)

    def scores(hd, j):
        k0 = pl.multiple_of(j * tk, tk)
        return jnp.dot(k_ref[0, pl.ds(k0, tk), hd * DV:(hd + 1) * DV], qbd_scr[hd], preferred_element_type=F32)

    def update(hd, s, j):
        m_prev = m_scr[hd]
        m_new = jnp.maximum(m_prev, jnp.max(s, axis=0, keepdims=True))
        alpha = jnp.exp2(m_prev - m_new)
        p = jnp.exp2(s - m_new).astype(BF16)
        acc_scr[hd] = alpha * acc_scr[hd] + jnp.dot(vt_ref[0, hd, j], p, preferred_element_type=F32)
        m_scr[hd] = m_new

    def past_tiles(j0, count):
        for t in range(count):
            for hd in heads:
                s_scrs[2 * hd + 1 - t % 2][...] = scores(hd, j0 + t + 1)
                update(hd, s_scrs[2 * hd + t % 2][...], j0 + t)

    for hd in heads:
        qbd_scr[hd] = jnp.concatenate([qt_ref[0, hd, 0], qt_ref[0, hd, 1]], axis=1)
        acc_scr[hd] = jnp.zeros((DV + ONES_ROWS, 2 * tq), F32)
        m_scr[hd] = jnp.full((1, 2 * tq), -jnp.inf, F32)
        s_scrs[2 * hd][...] = scores(hd, 0)

    def body(jj, carry):
        past_tiles(unroll * jj, unroll)
        return carry

    n_main = n_full // unroll
    lax.fori_loop(0, n_main, body, 0)

    if unroll > 2:
        def rest_body(jj, carry):
            past_tiles(n_main * unroll + 2 * jj, 2)
            return carry

        lax.fori_loop(0, (n_full - n_main * unroll) // 2, rest_body, 0)

    key = lax.broadcasted_iota(jnp.int32, (tk, 2 * tq), 0)
    qry = lax.broadcasted_iota(jnp.int32, (tk, 2 * tq), 1)
    qry = jnp.where(qry >= tq, qry - tq, qry)
    for d in range(n_diag):
        for hd in heads:
            if d + 1 < n_diag:
                s_scrs[2 * hd + (d + 1) % 2][...] = scores(hd, n_full + d + 1)
            update(hd, jnp.where(key + d * tk <= qry, s_scrs[2 * hd + d % 2][...], -jnp.inf), n_full + d)

    lam = _lam(lamp_ref)
    for hd in heads:
        ot = acc_scr[hd, 0:DV, :] / acc_scr[hd, DV:DV + 1, :]
        a = ot[:, 0:tq] - lam * ot[:, tq:2 * tq]
        ms = jnp.mean(a * a, axis=0, keepdims=True)
        o_ref[0, :, hd * DV:(hd + 1) * DV] = (
            (a * lax.rsqrt(ms + EPS)).T * sg_ref[...] * (1.0 - LAM_INIT)).astype(BF16)


def _attention(lamp, qt, kb, vt, subln_g, tq, unroll, hpb):
    B, S, _ = kb.shape
    tk = vt.shape[-1]
    assert tq == 2 * tk and S % tq == 0 and unroll % 2 == 0 and N_HEADS % hpb == 0
    return pl.pallas_call(
        functools.partial(_attn_kernel, tq=tq, tk=tk, unroll=unroll, hpb=hpb),
        grid=(B, N_HEADS // hpb, S // tq),
        in_specs=[pl.BlockSpec((4, DK), lambda b, h, i: (0, 0)),
                  pl.BlockSpec((1, hpb, 2, DV, tq), lambda b, h, i: (b, h, 0, 0, i)),
                  pl.BlockSpec((1, S, hpb * DV), lambda b, h, i: (b, 0, h)),
                  pl.BlockSpec((1, hpb, S // tk, DV + ONES_ROWS, tk), lambda b, h, i: (b, h, 0, 0, 0)),
                  pl.BlockSpec((1, DV), lambda b, h, i: (0, 0))],
        out_specs=pl.BlockSpec((1, tq, hpb * DV), lambda b, h, i: (b, i, h)),
        out_shape=jax.ShapeDtypeStruct((B, S, N_HEADS * DV), BF16),
        scratch_shapes=[pltpu.VMEM((hpb, DV, 2 * tq), BF16),
                        pltpu.VMEM((hpb, DV + ONES_ROWS, 2 * tq), F32),
                        pltpu.VMEM((hpb, 1, 2 * tq), F32)]
                       + [pltpu.VMEM((tk, 2 * tq), F32)] * (2 * hpb),
        compiler_params=pltpu.CompilerParams(dimension_semantics=("arbitrary", "arbitrary", "arbitrary"),
                                             vmem_limit_bytes=VMEM_LIMIT),
        name="attn",
    )(lamp, qt, kb, vt, subln_g)


def _sproj_kernel(x_ref, scale_ref, shift_ref, g_ref, w_ref, z_ref, wb_ref, h_scr):
    @pl.when(pl.program_id(0) == 0)
    def _():
        h_scr[...] = _modulated_norm(x_ref[...], g_ref[...], scale_ref[...], shift_ref[...]).astype(BF16)

    wb = w_ref[...].astype(BF16)
    wb_ref[...] = wb
    z_ref[...] = jnp.dot(h_scr[...], wb, preferred_element_type=F32)


def _sproj(x, scale, shift, norm_g, w_in):
    n = x.shape[0]
    full = pl.BlockSpec((n, D_MODEL), lambda j: (0, 0))
    wseg = pl.BlockSpec((D_MODEL, D_MODEL), lambda j: (0, j))
    return pl.pallas_call(
        _sproj_kernel,
        grid=(N_SEG,),
        in_specs=[full, full, full, pl.BlockSpec((1, D_MODEL), lambda j: (0, 0)), wseg],
        out_specs=[pl.BlockSpec((n, D_MODEL), lambda j: (0, j)), wseg],
        out_shape=[jax.ShapeDtypeStruct((n, D_IN), F32), jax.ShapeDtypeStruct((D_MODEL, D_IN), BF16)],
        scratch_shapes=[pltpu.VMEM((n, D_MODEL), BF16)],
        compiler_params=pltpu.CompilerParams(dimension_semantics=("arbitrary",)),
        name="sproj",
    )(x, scale, shift, norm_g, w_in)


def _sconv_kernel(za_ref, zb_ref, zc_ref, st_ref, cw_ref, cb_ref, lng_ref, lnb_ref,
                  ca_ref, cst_ref, u_scr, uext_scr, y_scr, *, n_seq, t_new):
    u_scr[...] = za_ref[...] * jax.nn.sigmoid(zb_ref[...])

    def one_seq(b, carry):
        r0 = pl.multiple_of(b * t_new, t_new)
        uext_scr[0:HALO, :] = st_ref[b]
        uext_scr[HALO:HALO + t_new, :] = u_scr[pl.ds(r0, t_new), :]
        blk = uext_scr[...]
        y_scr[pl.ds(r0, t_new), :] = _conv_rows(blk, cw_ref[...], t_new)
        cst_ref[b] = blk[HALO + t_new - (CONV_W - 1):HALO + t_new]
        return carry

    lax.fori_loop(0, n_seq, one_seq, 0)
    ca_ref[...] = _layernorm_gate(y_scr[...] + cb_ref[...], lng_ref[...], lnb_ref[...],
                                  zc_ref[...]).astype(BF16)


def _sconv(z, state, conv_w, conv_b, ln_g, ln_b, n_seq, t_new):
    n = n_seq * t_new
    zseg = lambda j: pl.BlockSpec((n, D_MODEL), lambda i: (0, j))
    const = lambda shape: pl.BlockSpec(shape, lambda i: (0,) * len(shape))
    return pl.pallas_call(
        functools.partial(_sconv_kernel, n_seq=n_seq, t_new=t_new),
        grid=(1,),
        in_specs=[zseg(4), zseg(5), zseg(6), const((n_seq, HALO, D_MODEL)),
                  const((CONV_W, D_MODEL)), const((1, D_MODEL)), const((1, D_MODEL)), const((1, D_MODEL))],
        out_specs=[const((n, D_MODEL)), const((n_seq, CONV_W - 1, D_MODEL))],
        out_shape=[jax.ShapeDtypeStruct((n, D_MODEL), BF16),
                   jax.ShapeDtypeStruct((n_seq, CONV_W - 1, D_MODEL), F32)],
        scratch_shapes=[pltpu.VMEM((n, D_MODEL), F32),
                        pltpu.VMEM((HALO + t_new, D_MODEL), F32),
                        pltpu.VMEM((n, D_MODEL), F32)],
        name="sconv",
    )(z, z, z, state, conv_w, conv_b, ln_g, ln_b)


def _decode_kernel(pt_ref, lamp_ref, q_ref, kn_ref, vn_ref, sg_ref, *rest, pp, t_new, n_groups):
    del pt_ref
    k_refs, v_refs = rest[:pp], rest[pp:2 * pp]
    o_ref, qrow_scr, bias_scr, m_scr, l_scr, acc_scr, s_scr = rest[2 * pp:]
    step = pl.program_id(1)
    n_rows = N_HEADS * 2 * t_new
    page_rows = PAGE * N_HEADS
    new_rows = t_new * N_HEADS
    log2_heads = N_HEADS.bit_length() - 1
    log2_pair = (2 * t_new).bit_length() - 1
    contract_last = (((1,), (1,)), ((), ()))

    def online_update(s, v, first):
        s_max = jnp.max(s, axis=1, keepdims=True)
        if first:
            m_new = s_max
            p = jnp.exp2(s - m_new)
            l_scr[...] = jnp.sum(p, axis=1, keepdims=True)
            acc_scr[...] = jnp.dot(p.astype(BF16), v, preferred_element_type=F32)
        else:
            m_prev = m_scr[...]
            m_new = jnp.maximum(m_prev, s_max)
            alpha = jnp.exp2(m_prev - m_new)
            p = jnp.exp2(s - m_new)
            l_scr[...] = alpha * l_scr[...] + jnp.sum(p, axis=1, keepdims=True)
            acc_scr[...] = alpha * acc_scr[...] + jnp.dot(p.astype(BF16), v, preferred_element_type=F32)
        m_scr[...] = m_new

    @pl.when(step == 0)
    def _():
        q = q_ref[...] * Q_SCALE
        lane = lax.broadcasted_iota(jnp.int32, (t_new, DV), 1)
        rows = []
        for h in range(N_HEADS):
            qh = q[:, h * DV:(h + 1) * DV]
            rows += [jnp.where(lane < DK, qh, 0.0), jnp.where(lane >= DK, qh, 0.0)]
        qrow_scr[...] = jnp.concatenate(rows, axis=0).astype(BF16)

        row = lax.broadcasted_iota(jnp.int32, (n_rows, page_rows), 0)
        col = lax.broadcasted_iota(jnp.int32, (n_rows, page_rows), 1)
        same_head = (col & (N_HEADS - 1)) == (row >> log2_pair)
        bias_scr[...] = jnp.where(same_head, 0.0, -jnp.inf)

        pad = jnp.zeros((PAGE - new_rows, DV), F32)
        kn = jnp.concatenate([kn_ref[...], pad], axis=0).astype(BF16)
        vn = jnp.concatenate([vn_ref[...], pad], axis=0).astype(BF16)
        s = lax.dot_general(qrow_scr[...], kn, contract_last, preferred_element_type=F32)
        row = lax.broadcasted_iota(jnp.int32, (n_rows, PAGE), 0)
        col = lax.broadcasted_iota(jnp.int32, (n_rows, PAGE), 1)
        visible = ((col & (N_HEADS - 1)) == (row >> log2_pair)) & ((col >> log2_heads) <= (row & (t_new - 1)))
        online_update(jnp.where(visible, s, -jnp.inf), vn, True)

    q_rows = qrow_scr[...]
    bias = bias_scr[...]

    def page_scores(i):
        s_scr[:, i * page_rows:(i + 1) * page_rows] = lax.dot_general(
            q_rows, k_refs[i][0].astype(BF16), contract_last, preferred_element_type=F32) + bias

    per_group = pp // n_groups
    for i in range(per_group):
        page_scores(i)
    for g in range(n_groups):
        if g + 1 < n_groups:
            for i in range((g + 1) * per_group, (g + 2) * per_group):
                page_scores(i)
        pages = slice(g * per_group, (g + 1) * per_group)
        v = jnp.concatenate([r[0] for r in v_refs[pages]], axis=0).astype(BF16)
        online_update(s_scr[:, g * per_group * page_rows:(g + 1) * per_group * page_rows], v, False)

    @pl.when(step == pl.num_programs(1) - 1)
    def _():
        lam = _lam(lamp_ref)
        o = acc_scr[...] / l_scr[...]
        for h in range(N_HEADS):
            r = h * 2 * t_new
            a = o[r:r + t_new] - lam * o[r + t_new:r + 2 * t_new]
            o_ref[:, h * DV:(h + 1) * DV] = _subln(a, sg_ref[...])


def _decode(page_table, lamp, q, k_new, v_new, cache_k, cache_v, subln_g, n_seq, t_new, pp):
    n_pages = page_table.shape[1]
    width = N_HEADS * DV
    page_rows = PAGE * N_HEADS
    new_rows = t_new * N_HEADS
    n_rows = N_HEADS * 2 * t_new
    assert t_new & (t_new - 1) == 0 and N_HEADS & (N_HEADS - 1) == 0
    assert n_pages % pp == 0 and new_rows <= PAGE
    page = lambda i: pl.BlockSpec((1, page_rows, DV), lambda b, s, pt: (pt[b, s * pp + i], 0, 0))
    new = lambda: pl.BlockSpec((new_rows, DV), lambda b, s, pt: (b, 0))
    grid_spec = pltpu.PrefetchScalarGridSpec(
        num_scalar_prefetch=1,
        grid=(n_seq, n_pages // pp),
        in_specs=[pl.BlockSpec((4, DK), lambda b, s, pt: (0, 0)),
                  pl.BlockSpec((t_new, width), lambda b, s, pt: (b, 0)), new(), new(),
                  pl.BlockSpec((1, DV), lambda b, s, pt: (0, 0))]
                 + [page(i) for i in range(pp)] + [page(i) for i in range(pp)],
        out_specs=pl.BlockSpec((t_new, width), lambda b, s, pt: (b, 0)),
        scratch_shapes=[pltpu.VMEM((n_rows, DV), BF16),
                        pltpu.VMEM((n_rows, page_rows), F32),
                        pltpu.VMEM((n_rows, 1), F32),
                        pltpu.VMEM((n_rows, 1), F32),
                        pltpu.VMEM((n_rows, DV), F32),
                        pltpu.VMEM((n_rows, pp * page_rows), F32)])
    return pl.pallas_call(
        functools.partial(_decode_kernel, pp=pp, t_new=t_new, n_groups=pp // DECODE_GROUP),
        grid_spec=grid_spec,
        out_shape=jax.ShapeDtypeStruct((n_seq * t_new, width), F32),
        compiler_params=pltpu.CompilerParams(dimension_semantics=("arbitrary", "arbitrary"),
                                             vmem_limit_bytes=VMEM_LIMIT),
        name="decode",
    )(page_table, lamp, q, k_new, v_new, subln_g, *([cache_k] * pp), *([cache_v] * pp))


def _post_kernel(o_ref, ga_ref, ca_ref, gma_ref, gmc_ref, x_ref, gate_ref,
                 wa_ref, wc_ref, bc_ref, wo_ref, fg_ref, y_ref):
    a = (o_ref[0].astype(F32) * _silu(ga_ref[0].astype(F32))).astype(BF16)
    ya = jnp.dot(a, wa_ref[...], preferred_element_type=F32)
    yc = jnp.dot(ca_ref[0], wc_ref[...], preferred_element_type=F32) + bc_ref[...]
    m = (jax.nn.sigmoid(gma_ref[0].astype(F32)) * ya + jax.nn.sigmoid(gmc_ref[0].astype(F32)) * yc)
    d = jnp.dot(m.astype(BF16), wo_ref[...], preferred_element_type=F32)
    xo = x_ref[0] + gate_ref[0] * d
    ms = jnp.mean(xo * xo, axis=-1, keepdims=True)
    y_ref[0] = xo * lax.rsqrt(ms + EPS) * fg_ref[...]


def _post(o, ga, ga_seg, ca, gm, gm_seg, x, gate, w_attn_o, w_conv_o, b_conv_o, w_out, final_g, tm):
    B, S, _ = x.shape
    gate_rows = gate.shape[1]
    tile = lambda seg=0: pl.BlockSpec((1, tm, D_MODEL), lambda b, t: (b, t, seg))
    const = lambda shape: pl.BlockSpec(shape, lambda b, t: (0,) * len(shape))
    if gate_rows == 1:
        gate_spec = pl.BlockSpec((1, 1, D_MODEL), lambda b, t: (b, 0, 0))
    else:
        gate_spec = tile()
    return pl.pallas_call(
        _post_kernel,
        grid=(B, S // tm),
        in_specs=[tile(), tile(ga_seg), tile(), tile(gm_seg), tile(gm_seg + 1), tile(), gate_spec,
                  const((D_MODEL, D_MODEL)), const((D_MODEL, D_MODEL)), const((1, D_MODEL)),
                  const((D_MODEL, D_MODEL)), const((1, D_MODEL))],
        out_specs=tile(),
        out_shape=jax.ShapeDtypeStruct((B, S, D_MODEL), F32),
        compiler_params=pltpu.CompilerParams(dimension_semantics=("arbitrary", "arbitrary"),
                                             vmem_limit_bytes=VMEM_LIMIT),
        name="post",
    )(o, ga, ca, gm, gm, x, gate, w_attn_o, w_conv_o, b_conv_o, w_out, final_g)


def kernel(x_prompt, x_sample, cache_k, cache_v, state_conv, page_table, c_prompt, c_sample, w_ada, b_ada, norm_g, w_in, lambda_q1, lambda_k1, lambda_q2, lambda_k2, subln_g, w_attn_o, conv_w, conv_b, conv_ln_g, conv_ln_b, w_conv_o, b_conv_o, w_out, final_g):
    assert w_ada.shape[0] == 1, "single-layer stack"
    B, S, _ = x_prompt.shape
    n_seq, t_new, _ = x_sample.shape
    n_tok = n_seq * t_new
    width = N_HEADS * DV

    row = lambda p: p.reshape(1, -1)
    w_attn_o_b, w_conv_o_b, w_out_b = (w[0].astype(BF16) for w in (w_attn_o, w_conv_o, w_out))
    norm_g0, conv_b0, ln_g0, ln_b0 = row(norm_g[0]), row(conv_b[0]), row(conv_ln_g[0]), row(conv_ln_b[0])
    b_conv_o0, final_g0, subln_g0 = row(b_conv_o[0]), row(final_g), row(subln_g[0])
    lamp = jnp.stack([lambda_q1[0], lambda_k1[0], lambda_q2[0], lambda_k2[0]])

    n_c = B + n_seq
    pad = -n_c % 8
    c_all = jnp.concatenate([c_prompt, c_sample, jnp.zeros((pad, D_MODEL), F32)], axis=0)
    mod = _ada(c_all, w_ada[0], row(b_ada[0]))
    shift, scale, gate = jnp.split(mod[:n_c], 3, axis=-1)

    s_mod = lambda m: jnp.repeat(m[B:], t_new, axis=0)
    xs = x_sample.reshape(n_tok, D_MODEL)
    z, w_in_b = _sproj(xs, s_mod(scale), s_mod(shift), norm_g0, w_in[0])

    p_mod = lambda m: m[:B].reshape(B, 1, D_MODEL)
    qt, k, kb, v, vt, ga, ca, gm, conv_prompt = _front(
        x_prompt, p_mod(scale), p_mod(shift), norm_g0, w_in_b, conv_w[0], conv_b0, ln_g0, ln_b0, tm=FRONT_TM)
    o = _attention(lamp, qt, kb, vt, subln_g0, tq=ATTN_TQ, unroll=ATTN_UNROLL, hpb=ATTN_HEADS)
    y_prompt = _post(o, ga, 0, ca, gm, 0, x_prompt, p_mod(gate),
                     w_attn_o_b, w_conv_o_b, b_conv_o0, w_out_b, final_g0, tm=POST_TM)

    state = jnp.pad(state_conv[0], ((0, 0), (HALO - (CONV_W - 1), 0), (0, 0)))
    ca_s, conv_sample = _sconv(z, state, conv_w[0], conv_b0, ln_g0, ln_b0, n_seq, t_new)
    heads = lambda a, lead: a.reshape(1, *lead, N_HEADS, DV)
    k_sample = heads(z[:, width:2 * width], (n_seq, t_new))
    v_sample = heads(z[:, 2 * width:3 * width], (n_seq, t_new))
    n_phys = cache_k.shape[1]
    o_s = _decode(page_table, lamp, z, k_sample.reshape(-1, DV), v_sample.reshape(-1, DV),
                  cache_k[0].reshape(n_phys, PAGE * N_HEADS, DV), cache_v[0].reshape(n_phys, PAGE * N_HEADS, DV),
                  subln_g0, n_seq, t_new, pp=DECODE_PAGES)
    z3 = z.reshape(1, n_tok, D_IN)
    y_sample = _post(o_s.reshape(1, n_tok, width), z3, 3, ca_s.reshape(1, n_tok, D_MODEL), z3, 7,
                     xs.reshape(1, n_tok, D_MODEL), s_mod(gate).reshape(1, n_tok, D_MODEL),
                     w_attn_o_b, w_conv_o_b, b_conv_o0, w_out_b, final_g0, tm=n_tok)

    return (y_prompt, y_sample.reshape(n_seq, t_new, D_MODEL),
            heads(k, (B, S)), heads(v, (B, S)), conv_prompt[None],
            k_sample, v_sample, conv_sample[None])
```

```python
import functools
import math

import jax
import jax.numpy as jnp
from jax import lax
from jax.experimental import pallas as pl
from jax.experimental.pallas import tpu as pltpu

F32 = jnp.float32
BF16 = jnp.bfloat16

D_MODEL = 1024
N_HEADS = 8
DK = 64
DV = 2 * DK
CONV_W = 31
PAGE = 128
EPS = 1e-6
N_SEG = 9
D_IN = N_SEG * D_MODEL
LAM_INIT = 0.8 - 0.6 * math.exp(-0.3 * 0)
LOG2E = 1.4426950408889634
Q_SCALE = DK ** -0.5 * LOG2E

SUBLANES, LANES = 8, 128
ONES_ROWS = 2 * SUBLANES
HALO = 4 * SUBLANES
VMEM_LIMIT = 56 * 1024 * 1024

FRONT_TM = 256
CONV_ROWS = 128
ATTN_TQ = 2 * FRONT_TM
ATTN_HEADS = 2
ATTN_UNROLL = 8
POST_TM = 512
DECODE_PAGES = 16
DECODE_GROUP = 4
ADA_TN = 6 * LANES


def _silu(x):
    return x * jax.nn.sigmoid(x)


def _modulated_norm(x, g, scale, shift):
    ms = jnp.mean(x * x, axis=-1, keepdims=True)
    return (x * lax.rsqrt(ms + EPS) * g) * (1.0 + scale) + shift


def _layernorm_gate(y, ln_g, ln_b, gc):
    mu = jnp.mean(y, axis=-1, keepdims=True)
    yc = y - mu
    var = jnp.mean(yc * yc, axis=-1, keepdims=True)
    ln = yc * lax.rsqrt(var + EPS) * ln_g + ln_b
    return _silu(ln) * _silu(gc)


def _conv_rows(blk, cw, rc):
    base = HALO - (CONV_W - 1)
    acc = None
    n = rc + HALO
    for r in range(SUBLANES):
        sh = blk if r == 0 else pltpu.roll(blk, n - r, axis=0)
        for a in range(HALO // SUBLANES + 1):
            w = SUBLANES * a + r - base
            if 0 <= w < CONV_W:
                term = sh[SUBLANES * a:SUBLANES * a + rc] * cw[w:w + 1]
                acc = term if acc is None else acc + term
    return acc


def _lam(lamp_ref):
    lp = lamp_ref[...]
    a = jnp.sum(lp[0:1] * lp[1:2], axis=-1, keepdims=True)
    b = jnp.sum(lp[2:3] * lp[3:4], axis=-1, keepdims=True)
    return jnp.exp(a) - jnp.exp(b) + LAM_INIT


def _subln(a, sg):
    ms = jnp.mean(a * a, axis=-1, keepdims=True)
    return a * lax.rsqrt(ms + EPS) * sg * (1.0 - LAM_INIT)


def _ada_kernel(c_ref, w_ref, b_ref, o_ref):
    a = _silu(c_ref[...]).astype(BF16)
    o_ref[...] = jnp.dot(a, w_ref[...].astype(BF16), preferred_element_type=F32) + b_ref[...]


def _ada(c_all, w_ada, b_ada):
    rows = c_all.shape[0]
    tn = ADA_TN
    return pl.pallas_call(
        _ada_kernel,
        grid=(3 * D_MODEL // tn,),
        in_specs=[pl.BlockSpec((rows, D_MODEL), lambda j: (0, 0)),
                  pl.BlockSpec((D_MODEL, tn), lambda j: (0, j)),
                  pl.BlockSpec((1, tn), lambda j: (0, j))],
        out_specs=pl.BlockSpec((rows, tn), lambda j: (0, j)),
        out_shape=jax.ShapeDtypeStruct((rows, 3 * D_MODEL), F32),
        name="ada",
    )(c_all, w_ada, b_ada)


def _front_kernel(x_ref, scale_ref, shift_ref, g_ref, w_ref, cw_ref, cb_ref, lng_ref, lnb_ref,
                  qt_ref, k_ref, kb_ref, v_ref, vt_ref, ga_ref, ca_ref, gm_ref, cst_ref,
                  h_scr, uext_scr, y_scr, *, tm):
    t = pl.program_id(1)
    h_scr[...] = _modulated_norm(x_ref[0], g_ref[...], scale_ref[0], shift_ref[0]).astype(BF16)

    def seg(j, n=1):
        return jnp.dot(h_scr[...], w_ref[:, j * D_MODEL:(j + n) * D_MODEL], preferred_element_type=F32)

    @pl.when(t == 0)
    def _():
        uext_scr[0:HALO, :] = jnp.zeros((HALO, D_MODEL), F32)

    uext_scr[HALO:HALO + tm, :] = seg(4) * jax.nn.sigmoid(seg(5))

    rc = CONV_ROWS

    def conv_rows(ci, carry):
        r0 = pl.multiple_of(ci * rc, rc)
        for lc in range(D_MODEL // LANES):
            ls = slice(lc * LANES, (lc + 1) * LANES)
            y_scr[pl.ds(r0, rc), ls] = _conv_rows(uext_scr[pl.ds(r0, rc + HALO), ls], cw_ref[:, ls], rc)
        return carry

    lax.fori_loop(0, tm // rc, conv_rows, 0)

    zq = seg(0) * Q_SCALE
    depth = lax.broadcasted_iota(jnp.int32, (DV, tm), 0)
    for h in range(N_HEADS):
        qt = zq[:, h * DV:(h + 1) * DV].T
        qt_ref[0, h, 0] = jnp.where(depth < DK, qt, 0.0).astype(BF16)
        qt_ref[0, h, 1] = jnp.where(depth >= DK, qt, 0.0).astype(BF16)
    zk = seg(1)
    k_ref[0] = zk
    kb_ref[0] = zk.astype(BF16)
    zv = seg(2)
    v_ref[0] = zv
    extra = lax.broadcasted_iota(jnp.int32, (ONES_ROWS, tm), 0)
    ones_rows = jnp.where(extra == 0, 1.0, 0.0)
    for h in range(N_HEADS):
        vt_ref[0, h, 0] = jnp.concatenate([zv[:, h * DV:(h + 1) * DV].T, ones_rows], axis=0).astype(BF16)
    ga_ref[0] = seg(3).astype(BF16)
    gm_ref[0] = seg(7, 2).astype(BF16)
    ca_ref[0] = _layernorm_gate(y_scr[...] + cb_ref[...], lng_ref[...], lnb_ref[...], seg(6)).astype(BF16)

    last = uext_scr[tm:tm + HALO, :]
    uext_scr[0:HALO, :] = last

    @pl.when(t == pl.num_programs(1) - 1)
    def _():
        cst_ref[0] = last[HALO - (CONV_W - 1):HALO]


def _front(x, scale, shift, norm_g, w_in, conv_w, conv_b, ln_g, ln_b, tm):
    B, S, _ = x.shape
    tile = lambda width: pl.BlockSpec((1, tm, width), lambda b, t: (b, t, 0))
    row = lambda: pl.BlockSpec((1, 1, D_MODEL), lambda b, t: (b, 0, 0))
    const = lambda shape: pl.BlockSpec(shape, lambda b, t: (0,) * len(shape))
    bsd = lambda width, dt: jax.ShapeDtypeStruct((B, S, width), dt)
    return pl.pallas_call(
        functools.partial(_front_kernel, tm=tm),
        grid=(B, S // tm),
        in_specs=[tile(D_MODEL), row(), row(), const((1, D_MODEL)),
                  pl.BlockSpec((D_MODEL, D_IN), lambda b, t: (0, 0), pipeline_mode=pl.Buffered(1)),
                  const((CONV_W, D_MODEL)), const((1, D_MODEL)), const((1, D_MODEL)), const((1, D_MODEL))],
        out_specs=[pl.BlockSpec((1, N_HEADS, 2, DV, tm), lambda b, t: (b, 0, 0, 0, t)),
                   tile(D_MODEL), tile(D_MODEL), tile(D_MODEL),
                   pl.BlockSpec((1, N_HEADS, 1, DV + ONES_ROWS, tm), lambda b, t: (b, 0, t, 0, 0)),
                   tile(D_MODEL), tile(D_MODEL), tile(2 * D_MODEL),
                   pl.BlockSpec((1, CONV_W - 1, D_MODEL), lambda b, t: (b, 0, 0))],
        out_shape=[jax.ShapeDtypeStruct((B, N_HEADS, 2, DV, S), BF16),
                   bsd(D_MODEL, F32), bsd(D_MODEL, BF16), bsd(D_MODEL, F32),
                   jax.ShapeDtypeStruct((B, N_HEADS, S // tm, DV + ONES_ROWS, tm), BF16),
                   bsd(D_MODEL, BF16), bsd(D_MODEL, BF16), bsd(2 * D_MODEL, BF16),
                   jax.ShapeDtypeStruct((B, CONV_W - 1, D_MODEL), F32)],
        scratch_shapes=[pltpu.VMEM((tm, D_MODEL), BF16),
                        pltpu.VMEM((HALO + tm, D_MODEL), F32),
                        pltpu.VMEM((tm, D_MODEL), F32)],
        compiler_params=pltpu.CompilerParams(dimension_semantics=("arbitrary", "arbitrary"),
                                             vmem_limit_bytes=VMEM_LIMIT),
        name="front",
    )(x, scale, shift, norm_g, w_in, conv_w, conv_b, ln_g, ln_b)


def _attn_kernel(lamp_ref, qt_ref, k_ref, vt_ref, sg_ref, o_ref, qbd_scr, acc_scr, m_scr, *s_scrs,
                 tq, tk, unroll, hpb):
    i = pl.program_id(2)
    n_diag = tq // tk
    n_full = i * n_diag
    heads = range(hpb)

    def scores(hd, j):
        k0 = pl.multiple_of(j * tk, tk)
        return jnp.dot(k_ref[0, pl.ds(k0, tk), hd * DV:(hd + 1) * DV], qbd_scr[hd], preferred_element_type=F32)

    def update(hd, s, j):
        m_prev = m_scr[hd]
        m_new = jnp.maximum(m_prev, jnp.max(s, axis=0, keepdims=True))
        alpha = jnp.exp2(m_prev - m_new)
        p = jnp.exp2(s - m_new).astype(BF16)
        acc_scr[hd] = alpha * acc_scr[hd] + jnp.dot(vt_ref[0, hd, j], p, preferred_element_type=F32)
        m_scr[hd] = m_new

    def past_tiles(j0, count):
        for t in range(count):
            for hd in heads:
                s_scrs[2 * hd + 1 - t % 2][...] = scores(hd, j0 + t + 1)
                update(hd, s_scrs[2 * hd + t % 2][...], j0 + t)

    for hd in heads:
        qbd_scr[hd] = jnp.concatenate([qt_ref[0, hd, 0], qt_ref[0, hd, 1]], axis=1)
        acc_scr[hd] = jnp.zeros((DV + ONES_ROWS, 2 * tq), F32)
        m_scr[hd] = jnp.full((1, 2 * tq), -jnp.inf, F32)
        s_scrs[2 * hd][...] = scores(hd, 0)

    def body(jj, carry):
        past_tiles(unroll * jj, unroll)
        return carry

    n_main = n_full // unroll
    lax.fori_loop(0, n_main, body, 0)

    if unroll > 2:
        def rest_body(jj, carry):
            past_tiles(n_main * unroll + 2 * jj, 2)
            return carry

        lax.fori_loop(0, (n_full - n_main * unroll) // 2, rest_body, 0)

    key = lax.broadcasted_iota(jnp.int32, (tk, 2 * tq), 0)
    qry = lax.broadcasted_iota(jnp.int32, (tk, 2 * tq), 1)
    qry = jnp.where(qry >= tq, qry - tq, qry)
    for d in range(n_diag):
        for hd in heads:
            if d + 1 < n_diag:
                s_scrs[2 * hd + (d + 1) % 2][...] = scores(hd, n_full + d + 1)
            update(hd, jnp.where(key + d * tk <= qry, s_scrs[2 * hd + d % 2][...], -jnp.inf), n_full + d)

    lam = _lam(lamp_ref)
    for hd in heads:
        ot = acc_scr[hd, 0:DV, :] / acc_scr[hd, DV:DV + 1, :]
        a = ot[:, 0:tq] - lam * ot[:, tq:2 * tq]
        ms = jnp.mean(a * a, axis=0, keepdims=True)
        o_ref[0, :, hd * DV:(hd + 1) * DV] = (
            (a * lax.rsqrt(ms + EPS)).T * sg_ref[...] * (1.0 - LAM_INIT)).astype(BF16)


def _attention(lamp, qt, kb, vt, subln_g, tq, unroll, hpb):
    B, S, _ = kb.shape
    tk = vt.shape[-1]
    assert tq == 2 * tk and S % tq == 0 and unroll % 2 == 0 and N_HEADS % hpb == 0
    return pl.pallas_call(
        functools.partial(_attn_kernel, tq=tq, tk=tk, unroll=unroll, hpb=hpb),
        grid=(B, N_HEADS // hpb, S // tq),
        in_specs=[pl.BlockSpec((4, DK), lambda b, h, i: (0, 0)),
                  pl.BlockSpec((1, hpb, 2, DV, tq), lambda b, h, i: (b, h, 0, 0, i)),
                  pl.BlockSpec((1, S, hpb * DV), lambda b, h, i: (b, 0, h)),
                  pl.BlockSpec((1, hpb, S // tk, DV + ONES_ROWS, tk), lambda b, h, i: (b, h, 0, 0, 0)),
                  pl.BlockSpec((1, DV), lambda b, h, i: (0, 0))],
        out_specs=pl.BlockSpec((1, tq, hpb * DV), lambda b, h, i: (b, i, h)),
        out_shape=jax.ShapeDtypeStruct((B, S, N_HEADS * DV), BF16),
        scratch_shapes=[pltpu.VMEM((hpb, DV, 2 * tq), BF16),
                        pltpu.VMEM((hpb, DV + ONES_ROWS, 2 * tq), F32),
                        pltpu.VMEM((hpb, 1, 2 * tq), F32)]
                       + [pltpu.VMEM((tk, 2 * tq), F32)] * (2 * hpb),
        compiler_params=pltpu.CompilerParams(dimension_semantics=("arbitrary", "arbitrary", "arbitrary"),
                                             vmem_limit_bytes=VMEM_LIMIT),
        name="attn",
    )(lamp, qt, kb, vt, subln_g)


def _sproj_kernel(x_ref, scale_ref, shift_ref, g_ref, w_ref, z_ref, wb_ref, h_scr):
    @pl.when(pl.program_id(0) == 0)
    def _():
        h_scr[...] = _modulated_norm(x_ref[...], g_ref[...], scale_ref[...], shift_ref[...]).astype(BF16)

    wb = w_ref[...].astype(BF16)
    wb_ref[...] = wb
    z_ref[...] = jnp.dot(h_scr[...], wb, preferred_element_type=F32)


def _sproj(x, scale, shift, norm_g, w_in):
    n = x.shape[0]
    full = pl.BlockSpec((n, D_MODEL), lambda j: (0, 0))
    wseg = pl.BlockSpec((D_MODEL, D_MODEL), lambda j: (0, j))
    return pl.pallas_call(
        _sproj_kernel,
        grid=(N_SEG,),
        in_specs=[full, full, full, pl.BlockSpec((1, D_MODEL), lambda j: (0, 0)), wseg],
        out_specs=[pl.BlockSpec((n, D_MODEL), lambda j: (0, j)), wseg],
        out_shape=[jax.ShapeDtypeStruct((n, D_IN), F32), jax.ShapeDtypeStruct((D_MODEL, D_IN), BF16)],
        scratch_shapes=[pltpu.VMEM((n, D_MODEL), BF16)],
        compiler_params=pltpu.CompilerParams(dimension_semantics=("arbitrary",)),
        name="sproj",
    )(x, scale, shift, norm_g, w_in)


def _sconv_kernel(za_ref, zb_ref, zc_ref, st_ref, cw_ref, cb_ref, lng_ref, lnb_ref,
                  ca_ref, cst_ref, u_scr, uext_scr, y_scr, *, n_seq, t_new):
    u_scr[...] = za_ref[...] * jax.nn.sigmoid(zb_ref[...])

    def one_seq(b, carry):
        r0 = pl.multiple_of(b * t_new, t_new)
        uext_scr[0:HALO, :] = st_ref[b]
        uext_scr[HALO:HALO + t_new, :] = u_scr[pl.ds(r0, t_new), :]
        blk = uext_scr[...]
        y_scr[pl.ds(r0, t_new), :] = _conv_rows(blk, cw_ref[...], t_new)
        cst_ref[b] = blk[HALO + t_new - (CONV_W - 1):HALO + t_new]
        return carry

    lax.fori_loop(0, n_seq, one_seq, 0)
    ca_ref[...] = _layernorm_gate(y_scr[...] + cb_ref[...], lng_ref[...], lnb_ref[...],
                                  zc_ref[...]).astype(BF16)


def _sconv(z, state, conv_w, conv_b, ln_g, ln_b, n_seq, t_new):
    n = n_seq * t_new
    zseg = lambda j: pl.BlockSpec((n, D_MODEL), lambda i: (0, j))
    const = lambda shape: pl.BlockSpec(shape, lambda i: (0,) * len(shape))
    return pl.pallas_call(
        functools.partial(_sconv_kernel, n_seq=n_seq, t_new=t_new),
        grid=(1,),
        in_specs=[zseg(4), zseg(5), zseg(6), const((n_seq, HALO, D_MODEL)),
                  const((CONV_W, D_MODEL)), const((1, D_MODEL)), const((1, D_MODEL)), const((1, D_MODEL))],
        out_specs=[const((n, D_MODEL)), const((n_seq, CONV_W - 1, D_MODEL))],
        out_shape=[jax.ShapeDtypeStruct((n, D_MODEL), BF16),
                   jax.ShapeDtypeStruct((n_seq, CONV_W - 1, D_MODEL), F32)],
        scratch_shapes=[pltpu.VMEM((n, D_MODEL), F32),
                        pltpu.VMEM((HALO + t_new, D_MODEL), F32),
                        pltpu.VMEM((n, D_MODEL), F32)],
        name="sconv",
    )(z, z, z, state, conv_w, conv_b, ln_g, ln_b)


def _decode_kernel(pt_ref, lamp_ref, q_ref, kn_ref, vn_ref, sg_ref, *rest, pp, t_new, n_groups):
    del pt_ref
    k_refs, v_refs = rest[:pp], rest[pp:2 * pp]
    o_ref, qrow_scr, bias_scr, m_scr, l_scr, acc_scr, s_scr = rest[2 * pp:]
    step = pl.program_id(1)
    n_rows = N_HEADS * 2 * t_new
    page_rows = PAGE * N_HEADS
    new_rows = t_new * N_HEADS
    log2_heads = N_HEADS.bit_length() - 1
    log2_pair = (2 * t_new).bit_length() - 1
    contract_last = (((1,), (1,)), ((), ()))

    def online_update(s, v, first):
        s_max = jnp.max(s, axis=1, keepdims=True)
        if first:
            m_new = s_max
            p = jnp.exp2(s - m_new)
            l_scr[...] = jnp.sum(p, axis=1, keepdims=True)
            acc_scr[...] = jnp.dot(p.astype(BF16), v, preferred_element_type=F32)
        else:
            m_prev = m_scr[...]
            m_new = jnp.maximum(m_prev, s_max)
            alpha = jnp.exp2(m_prev - m_new)
            p = jnp.exp2(s - m_new)
            l_scr[...] = alpha * l_scr[...] + jnp.sum(p, axis=1, keepdims=True)
            acc_scr[...] = alpha * acc_scr[...] + jnp.dot(p.astype(BF16), v, preferred_element_type=F32)
        m_scr[...] = m_new

    @pl.when(step == 0)
    def _():
        q = q_ref[...] * Q_SCALE
        lane = lax.broadcasted_iota(jnp.int32, (t_new, DV), 1)
        rows = []
        for h in range(N_HEADS):
            qh = q[:, h * DV:(h + 1) * DV]
            rows += [jnp.where(lane < DK, qh, 0.0), jnp.where(lane >= DK, qh, 0.0)]
        qrow_scr[...] = jnp.concatenate(rows, axis=0).astype(BF16)

        row = lax.broadcasted_iota(jnp.int32, (n_rows, page_rows), 0)
        col = lax.broadcasted_iota(jnp.int32, (n_rows, page_rows), 1)
        same_head = (col & (N_HEADS - 1)) == (row >> log2_pair)
        bias_scr[...] = jnp.where(same_head, 0.0, -jnp.inf)

        pad = jnp.zeros((PAGE - new_rows, DV), F32)
        kn = jnp.concatenate([kn_ref[...], pad], axis=0).astype(BF16)
        vn = jnp.concatenate([vn_ref[...], pad], axis=0).astype(BF16)
        s = lax.dot_general(qrow_scr[...], kn, contract_last, preferred_element_type=F32)
        row = lax.broadcasted_iota(jnp.int32, (n_rows, PAGE), 0)
        col = lax.broadcasted_iota(jnp.int32, (n_rows, PAGE), 1)
        visible = ((col & (N_HEADS - 1)) == (row >> log2_pair)) & ((col >> log2_heads) <= (row & (t_new - 1)))
        online_update(jnp.where(visible, s, -jnp.inf), vn, True)

    q_rows = qrow_scr[...]
    bias = bias_scr[...]

    def page_scores(i):
        s_scr[:, i * page_rows:(i + 1) * page_rows] = lax.dot_general(
            q_rows, k_refs[i][0].astype(BF16), contract_last, preferred_element_type=F32) + bias

    per_group = pp // n_groups
    for i in range(per_group):
        page_scores(i)
    for g in range(n_groups):
        if g + 1 < n_groups:
            for i in range((g + 1) * per_group, (g + 2) * per_group):
                page_scores(i)
        pages = slice(g * per_group, (g + 1) * per_group)
        v = jnp.concatenate([r[0] for r in v_refs[pages]], axis=0).astype(BF16)
        online_update(s_scr[:, g * per_group * page_rows:(g + 1) * per_group * page_rows], v, False)

    @pl.when(step == pl.num_programs(1) - 1)
    def _():
        lam = _lam(lamp_ref)
        o = acc_scr[...] / l_scr[...]
        for h in range(N_HEADS):
            r = h * 2 * t_new
            a = o[r:r + t_new] - lam * o[r + t_new:r + 2 * t_new]
            o_ref[:, h * DV:(h + 1) * DV] = _subln(a, sg_ref[...])


def _decode(page_table, lamp, q, k_new, v_new, cache_k, cache_v, subln_g, n_seq, t_new, pp):
    n_pages = page_table.shape[1]
    width = N_HEADS * DV
    page_rows = PAGE * N_HEADS
    new_rows = t_new * N_HEADS
    n_rows = N_HEADS * 2 * t_new
    assert t_new & (t_new - 1) == 0 and N_HEADS & (N_HEADS - 1) == 0
    assert n_pages % pp == 0 and new_rows <= PAGE
    page = lambda i: pl.BlockSpec((1, page_rows, DV), lambda b, s, pt: (pt[b, s * pp + i], 0, 0))
    new = lambda: pl.BlockSpec((new_rows, DV), lambda b, s, pt: (b, 0))
    grid_spec = pltpu.PrefetchScalarGridSpec(
        num_scalar_prefetch=1,
        grid=(n_seq, n_pages // pp),
        in_specs=[pl.BlockSpec((4, DK), lambda b, s, pt: (0, 0)),
                  pl.BlockSpec((t_new, width), lambda b, s, pt: (b, 0)), new(), new(),
                  pl.BlockSpec((1, DV), lambda b, s, pt: (0, 0))]
                 + [page(i) for i in range(pp)] + [page(i) for i in range(pp)],
        out_specs=pl.BlockSpec((t_new, width), lambda b, s, pt: (b, 0)),
        scratch_shapes=[pltpu.VMEM((n_rows, DV), BF16),
                        pltpu.VMEM((n_rows, page_rows), F32),
                        pltpu.VMEM((n_rows, 1), F32),
                        pltpu.VMEM((n_rows, 1), F32),
                        pltpu.VMEM((n_rows, DV), F32),
                        pltpu.VMEM((n_rows, pp * page_rows), F32)])
    return pl.pallas_call(
        functools.partial(_decode_kernel, pp=pp, t_new=t_new, n_groups=pp // DECODE_GROUP),
        grid_spec=grid_spec,
        out_shape=jax.ShapeDtypeStruct((n_seq * t_new, width), F32),
        compiler_params=pltpu.CompilerParams(dimension_semantics=("arbitrary", "arbitrary"),
                                             vmem_limit_bytes=VMEM_LIMIT),
        name="decode",
    )(page_table, lamp, q, k_new, v_new, subln_g, *([cache_k] * pp), *([cache_v] * pp))


def _post_kernel(o_ref, ga_ref, ca_ref, gma_ref, gmc_ref, x_ref, gate_ref,
                 wa_ref, wc_ref, bc_ref, wo_ref, fg_ref, y_ref):
    a = (o_ref[0].astype(F32) * _silu(ga_ref[0].astype(F32))).astype(BF16)
    ya = jnp.dot(a, wa_ref[...], preferred_element_type=F32)
    yc = jnp.dot(ca_ref[0], wc_ref[...], preferred_element_type=F32) + bc_ref[...]
    m = (jax.nn.sigmoid(gma_ref[0].astype(F32)) * ya + jax.nn.sigmoid(gmc_ref[0].astype(F32)) * yc)
    d = jnp.dot(m.astype(BF16), wo_ref[...], preferred_element_type=F32)
    xo = x_ref[0] + gate_ref[0] * d
    ms = jnp.mean(xo * xo, axis=-1, keepdims=True)
    y_ref[0] = xo * lax.rsqrt(ms + EPS) * fg_ref[...]


def _post(o, ga, ga_seg, ca, gm, gm_seg, x, gate, w_attn_o, w_conv_o, b_conv_o, w_out, final_g, tm):
    B, S, _ = x.shape
    gate_rows = gate.shape[1]
    tile = lambda seg=0: pl.BlockSpec((1, tm, D_MODEL), lambda b, t: (b, t, seg))
    const = lambda shape: pl.BlockSpec(shape, lambda b, t: (0,) * len(shape))
    if gate_rows == 1:
        gate_spec = pl.BlockSpec((1, 1, D_MODEL), lambda b, t: (b, 0, 0))
    else:
        gate_spec = tile()
    return pl.pallas_call(
        _post_kernel,
        grid=(B, S // tm),
        in_specs=[tile(), tile(ga_seg), tile(), tile(gm_seg), tile(gm_seg + 1), tile(), gate_spec,
                  const((D_MODEL, D_MODEL)), const((D_MODEL, D_MODEL)), const((1, D_MODEL)),
                  const((D_MODEL, D_MODEL)), const((1, D_MODEL))],
        out_specs=tile(),
        out_shape=jax.ShapeDtypeStruct((B, S, D_MODEL), F32),
        compiler_params=pltpu.CompilerParams(dimension_semantics=("arbitrary", "arbitrary"),
                                             vmem_limit_bytes=VMEM_LIMIT),
        name="post",
    )(o, ga, ca, gm, gm, x, gate, w_attn_o, w_conv_o, b_conv_o, w_out, final_g)


def kernel(x_prompt, x_sample, cache_k, cache_v, state_conv, page_table, c_prompt, c_sample, w_ada, b_ada, norm_g, w_in, lambda_q1, lambda_k1, lambda_q2, lambda_k2, subln_g, w_attn_o, conv_w, conv_b, conv_ln_g, conv_ln_b, w_conv_o, b_conv_o, w_out, final_g):
    assert w_ada.shape[0] == 1, "single-layer stack"
    B, S, _ = x_prompt.shape
    n_seq, t_new, _ = x_sample.shape
    n_tok = n_seq * t_new
    width = N_HEADS * DV

    row = lambda p: p.reshape(1, -1)
    w_attn_o_b, w_conv_o_b, w_out_b = (w[0].astype(BF16) for w in (w_attn_o, w_conv_o, w_out))
    norm_g0, conv_b0, ln_g0, ln_b0 = row(norm_g[0]), row(conv_b[0]), row(conv_ln_g[0]), row(conv_ln_b[0])
    b_conv_o0, final_g0, subln_g0 = row(b_conv_o[0]), row(final_g), row(subln_g[0])
    lamp = jnp.stack([lambda_q1[0], lambda_k1[0], lambda_q2[0], lambda_k2[0]])

    n_c = B + n_seq
    pad = -n_c % 8
    c_all = jnp.concatenate([c_prompt, c_sample, jnp.zeros((pad, D_MODEL), F32)], axis=0)
    mod = _ada(c_all, w_ada[0], row(b_ada[0]))
    shift, scale, gate = jnp.split(mod[:n_c], 3, axis=-1)

    s_mod = lambda m: jnp.repeat(m[B:], t_new, axis=0)
    xs = x_sample.reshape(n_tok, D_MODEL)
    z, w_in_b = _sproj(xs, s_mod(scale), s_mod(shift), norm_g0, w_in[0])

    p_mod = lambda m: m[:B].reshape(B, 1, D_MODEL)
    qt, k, kb, v, vt, ga, ca, gm, conv_prompt = _front(
        x_prompt, p_mod(scale), p_mod(shift), norm_g0, w_in_b, conv_w[0], conv_b0, ln_g0, ln_b0, tm=FRONT_TM)
    o = _attention(lamp, qt, kb, vt, subln_g0, tq=ATTN_TQ, unroll=ATTN_UNROLL, hpb=ATTN_HEADS)
    y_prompt = _post(o, ga, 0, ca, gm, 0, x_prompt, p_mod(gate),
                     w_attn_o_b, w_conv_o_b, b_conv_o0, w_out_b, final_g0, tm=POST_TM)

    state = jnp.pad(state_conv[0], ((0, 0), (HALO - (CONV_W - 1), 0), (0, 0)))
    ca_s, conv_sample = _sconv(z, state, conv_w[0], conv_b0, ln_g0, ln_b0, n_seq, t_new)
    heads = lambda a, lead: a.reshape(1, *lead, N_HEADS, DV)
    k_sample = heads(z[:, width:2 * width], (n_seq, t_new))
    v_sample = heads(z[:, 2 * width:3 * width], (n_seq, t_new))
    n_phys = cache_k.shape[1]
    o_s = _decode(page_table, lamp, z, k_sample.reshape(-1, DV), v_sample.reshape(-1, DV),
                  cache_k[0].reshape(n_phys, PAGE * N_HEADS, DV), cache_v[0].reshape(n_phys, PAGE * N_HEADS, DV),
                  subln_g0, n_seq, t_new, pp=DECODE_PAGES)
    z3 = z.reshape(1, n_tok, D_IN)
    y_sample = _post(o_s.reshape(1, n_tok, width), z3, 3, ca_s.reshape(1, n_tok, D_MODEL), z3, 7,
                     xs.reshape(1, n_tok, D_MODEL), s_mod(gate).reshape(1, n_tok, D_MODEL),
                     w_attn_o_b, w_conv_o_b, b_conv_o0, w_out_b, final_g0, tm=n_tok)

    return (y_prompt, y_sample.reshape(n_seq, t_new, D_MODEL),
            heads(k, (B, S)), heads(v, (B, S)), conv_prompt[None],
            k_sample, v_sample, conv_sample[None])
```
